```python
import math
import jax
import jax.numpy as jnp
from jax import lax
import numpy as np

D_MODEL = 4096
BATCH = 4
SEQ = 2048
DEPTH = 1
DEC_BATCH = 32
DEC_SEQ = 8
PAST_LEN = 8192
PAGE_SIZE = 128

HEAD_DIM = 128
N_GROUPS = 3
HEADS_PER_GROUP = 8
N_HEADS_ATT = N_GROUPS * HEADS_PER_GROUP
ATT_W = N_HEADS_ATT * HEAD_DIM
ATT_OUT = HEADS_PER_GROUP * HEAD_DIM
WINDOWS = (128, 512, 2048)
DILATIONS = (1, 4, 16)
ATT_SPAN = 128
ATT_QBLOCK = 128
CHUNK = 128
D_SG = D_MODEL // 2
SG_GROUPS = 8
SG_CH = D_SG // SG_GROUPS
PEER_HEADS = 8
PEER_NKEYS = 128
PEER_EXPERTS = PEER_NKEYS * PEER_NKEYS
PEER_QDIM = 256
PEER_HALF = PEER_QDIM // 2
PEER_TOPK = 16
PEER_BLOCK = 64
N_MOD = 6
EPS = 1e-6
IN_COLS = 3 * ATT_W + 2 * D_SG + 2 * D_MODEL
IN_SPLITS = (ATT_W, 2 * ATT_W, 3 * ATT_W, 3 * ATT_W + D_SG, 3 * ATT_W + 2 * D_SG,
             3 * ATT_W + 2 * D_SG + D_MODEL)

kernel_name = 'hybrid_dilated_gmlp_peer_decoder_step'


def rmsnorm(x, g):
    xf = x.astype(jnp.float32)
    y = xf * lax.rsqrt(jnp.mean(xf * xf, axis=-1, keepdims=True) + EPS)
    return (y * g.astype(jnp.float32)).astype(x.dtype)


def layernorm(x, g, b):
    xf = x.astype(jnp.float32)
    mu = jnp.mean(xf, axis=-1, keepdims=True)
    xc = xf - mu
    y = xc * lax.rsqrt(jnp.mean(xc * xc, axis=-1, keepdims=True) + EPS)
    return (y * g.astype(jnp.float32) + b.astype(jnp.float32)).astype(x.dtype)


def alibi_slopes():
    i = jnp.arange(1, N_HEADS_ATT + 1, dtype=jnp.float32)
    return jnp.exp2(-8.0 * i / N_HEADS_ATT).reshape(N_GROUPS, HEADS_PER_GROUP)


def dilated_attention(q, k_ext, v_ext, offset, slopes, dil):
    B, T, H, hd = q.shape
    qb = ATT_QBLOCK if T % ATT_QBLOCK == 0 else T
    dist = jnp.arange(ATT_SPAN + 1, dtype=jnp.int32) * dil
    bias = -slopes[:, None] * dist.astype(jnp.float32)[None, :]
    scale = HEAD_DIM ** -0.5

    def one_block(t0):
        q_blk = lax.dynamic_slice_in_dim(q, t0, qb, axis=1)
        kidx = offset + t0 + jnp.arange(qb, dtype=jnp.int32)[:, None] - dist[None, :]
        valid = kidx >= 0
        kidx = jnp.maximum(kidx, 0)
        k_g = k_ext[:, kidx]
        v_g = v_ext[:, kidx]
        s = jnp.einsum('bqhd,bqjhd->bhqj', q_blk, k_g).astype(jnp.float32) * scale
        s = jnp.where(valid[None, None], s + bias[None, :, None, :], -jnp.inf)
        m = jnp.max(s, axis=-1, keepdims=True)
        p = jnp.exp(s - m)
        den = jnp.sum(p, axis=-1, keepdims=True)
        o = jnp.einsum('bhqj,bqjhd->bqhd', p / den, v_g.astype(jnp.float32))
        lse = jnp.transpose((m + jnp.log(den))[..., 0], (0, 2, 1))
        return o, lse

    starts = jnp.arange(T // qb, dtype=jnp.int32) * qb
    o, lse = lax.map(one_block, starts)
    o = jnp.transpose(o, (1, 0, 2, 3, 4)).reshape(B, T, H, hd)
    lse = jnp.transpose(lse, (1, 0, 2, 3)).reshape(B, T, H)
    return o, lse


def spatial_gate(v, w_s, b_s):
    B, T, _ = v.shape
    t_pad = -(-T // CHUNK) * CHUNK
    vp = jnp.pad(v, ((0, 0), (0, t_pad - T), (0, 0))).reshape(B, t_pad // CHUNK, CHUNK, SG_GROUPS, SG_CH)
    w_m = w_s * jnp.tril(jnp.ones((CHUNK, CHUNK), w_s.dtype))
    mixed = jnp.einsum('gts,bnsgc->bntgc', w_m, vp) + b_s.T[None, None, :, :, None]
    return mixed.reshape(B, t_pad, D_SG)[:, :T].astype(v.dtype)


def peer_ffn(h, w_q, sub_keys, u_tab, v_tab):
    B, T, D = h.shape
    n = B * T
    tb = math.gcd(n, PEER_BLOCK)

    def one_block(hx):
        q = (hx @ w_q).astype(jnp.float32).reshape(tb, PEER_HEADS, 2, PEER_HALF)
        s = jnp.einsum('nhpc,hpkc->nhpk', q, sub_keys.astype(jnp.float32))
        top_s, top_i = lax.top_k(s, PEER_TOPK)
        cand_s = (top_s[:, :, 0, :, None] + top_s[:, :, 1, None, :]).reshape(tb, PEER_HEADS, PEER_TOPK * PEER_TOPK)
        cand_i = (top_i[:, :, 0, :, None] * PEER_NKEYS + top_i[:, :, 1, None, :]).reshape(tb, PEER_HEADS, PEER_TOPK * PEER_TOPK)
        best_s, best_j = lax.top_k(cand_s, PEER_TOPK)
        expert = jnp.take_along_axis(cand_i, best_j, axis=-1)
        gate = jax.nn.softmax(best_s, axis=-1)
        act = jax.nn.gelu(jnp.einsum('nd,nhkd->nhk', hx, u_tab[expert]).astype(jnp.float32), approximate=False)
        return jnp.einsum('nhk,nhkd->nd', (gate * act).astype(hx.dtype), v_tab[expert])

    return lax.map(one_block, h.reshape(n // tb, tb, D)).reshape(B, T, D)


def decoder_layer(x, c, past_kv, ada_w, ada_b, norm1_g, norm2_g, w_in, sg_ln_g, sg_ln_b,
                  sg_w, sg_b, w_pa, w_pb, w_o, peer_wq, peer_subkeys, peer_u, peer_v):
    B, T, _ = x.shape
    mod = jax.nn.silu(c) @ ada_w + ada_b
    sh1, sc1, gt1, sh2, sc2, gt2 = jnp.split(mod[:, None, :], N_MOD, axis=-1)
    h = rmsnorm(x, norm1_g) * (1 + sc1) + sh1
    q, k, v, su, sv, ga, gb = jnp.split(h @ w_in, IN_SPLITS, axis=-1)
    q = q.reshape(B, T, N_GROUPS, HEADS_PER_GROUP, HEAD_DIM)
    k = k.reshape(B, T, N_GROUPS, HEADS_PER_GROUP, HEAD_DIM)
    v = v.reshape(B, T, N_GROUPS, HEADS_PER_GROUP, HEAD_DIM)
    slopes = alibi_slopes()
    outs, lses, new_kv = [], [], []
    for g in range(N_GROUPS):
        k_g, v_g = k[:, :, g], v[:, :, g]
        if past_kv is None:
            keep = min(WINDOWS[g], T)
            k_ext, v_ext, offset = k_g, v_g, 0
            new_kv.append((k_g[:, T - keep:], v_g[:, T - keep:]))
        else:
            k_buf, v_buf = past_kv[g]
            k_ext = jnp.concatenate([k_buf.astype(k_g.dtype), k_g], axis=1)
            v_ext = jnp.concatenate([v_buf.astype(v_g.dtype), v_g], axis=1)
            offset = k_buf.shape[1]
            new_kv.append((k_g, v_g))
        o_g, lse_g = dilated_attention(q[:, :, g], k_ext, v_ext, offset, slopes[g], DILATIONS[g])
        outs.append(o_g)
        lses.append(lse_g)
    w_grp = jax.nn.softmax(jnp.stack(lses, axis=0), axis=0)
    att = jnp.sum(w_grp[..., None] * jnp.stack(outs, axis=0), axis=0).reshape(B, T, ATT_OUT).astype(x.dtype)
    su = jax.nn.gelu(su, approximate=False)
    sv = layernorm(jax.nn.gelu(sv, approximate=False), sg_ln_g, sg_ln_b)
    sg = su * spatial_gate(sv, sg_w, sg_b)
    merged = jax.nn.sigmoid(ga) * (att @ w_pa) + jax.nn.sigmoid(gb) * (sg @ w_pb)
    x = x + gt1 * (merged @ w_o)
    h2 = rmsnorm(x, norm2_g) * (1 + sc2) + sh2
    x = x + gt2 * peer_ffn(h2, peer_wq, peer_subkeys, peer_u, peer_v)
    return x, new_kv, sv


def setup_inputs(seed: int = 0) -> dict:
    key = jax.random.key(seed)
    ks = jax.random.split(key, 32)
    f32 = jnp.float32

    def nrm(k, shape, scale):
        return jax.random.normal(k, shape, f32) * scale

    caches = []
    for g in range(N_GROUPS):
        wb = min(WINDOWS[g], PAST_LEN)
        shp = (DEPTH, DEC_BATCH, wb, HEADS_PER_GROUP, HEAD_DIM)
        caches.append(nrm(ks[2 + 2 * g], shp, 1.0))
        caches.append(nrm(ks[3 + 2 * g], shp, 1.0))
    return {
        'x_prompt': nrm(ks[0], (BATCH, SEQ, D_MODEL), 1.0),
        'x_sample': nrm(ks[1], (DEC_BATCH, DEC_SEQ, D_MODEL), 1.0),
        'cache_k0': caches[0], 'cache_v0': caches[1],
        'cache_k1': caches[2], 'cache_v1': caches[3],
        'cache_k2': caches[4], 'cache_v2': caches[5],
        'c_prompt': nrm(ks[8], (BATCH, D_MODEL), 1.0),
        'c_sample': nrm(ks[9], (DEC_BATCH, D_MODEL), 1.0),
        'ada_w': nrm(ks[10], (DEPTH, D_MODEL, N_MOD * D_MODEL), D_MODEL ** -0.5),
        'ada_b': nrm(ks[11], (DEPTH, N_MOD * D_MODEL), 0.02),
        'norm1_g': 1.0 + nrm(ks[12], (DEPTH, D_MODEL), 0.02),
        'norm2_g': 1.0 + nrm(ks[13], (DEPTH, D_MODEL), 0.02),
        'w_in': nrm(ks[14], (DEPTH, D_MODEL, IN_COLS), D_MODEL ** -0.5),
        'sg_ln_g': 1.0 + nrm(ks[15], (DEPTH, D_SG), 0.02),
        'sg_ln_b': nrm(ks[16], (DEPTH, D_SG), 0.02),
        'sg_w': nrm(ks[17], (DEPTH, SG_GROUPS, CHUNK, CHUNK), CHUNK ** -0.5),
        'sg_b': 1.0 + nrm(ks[18], (DEPTH, SG_GROUPS, CHUNK), 0.02),
        'w_pa': nrm(ks[19], (DEPTH, ATT_OUT, D_MODEL), ATT_OUT ** -0.5),
        'w_pb': nrm(ks[20], (DEPTH, D_SG, D_MODEL), D_SG ** -0.5),
        'w_o': nrm(ks[21], (DEPTH, D_MODEL, D_MODEL), D_MODEL ** -0.5),
        'peer_wq': nrm(ks[22], (DEPTH, D_MODEL, PEER_HEADS * PEER_QDIM), D_MODEL ** -0.5),
        'peer_subkeys': nrm(ks[23], (DEPTH, PEER_HEADS, 2, PEER_NKEYS, PEER_HALF), PEER_HALF ** -0.5),
        'peer_u': nrm(ks[24], (DEPTH, PEER_EXPERTS, D_MODEL), D_MODEL ** -0.5),
        'peer_v': nrm(ks[25], (DEPTH, PEER_EXPERTS, D_MODEL), PEER_HEADS ** -0.5),
        'final_g': 1.0 + nrm(ks[26], (D_MODEL,), 0.02),
    }


def reference(x_prompt, x_sample, cache_k0, cache_v0, cache_k1, cache_v1, cache_k2, cache_v2,
              c_prompt, c_sample, ada_w, ada_b, norm1_g, norm2_g, w_in, sg_ln_g, sg_ln_b,
              sg_w, sg_b, w_pa, w_pb, w_o, peer_wq, peer_subkeys, peer_u, peer_v, final_g):
    cache_k = (cache_k0, cache_k1, cache_k2)
    cache_v = (cache_v0, cache_v1, cache_v2)
    hp, hs = x_prompt, x_sample
    kp = [[] for _ in range(N_GROUPS)]
    vp = [[] for _ in range(N_GROUPS)]
    ks_ = [[] for _ in range(N_GROUPS)]
    vs_ = [[] for _ in range(N_GROUPS)]
    sgv = []
    for l in range(DEPTH):
        lp = (ada_w[l], ada_b[l], norm1_g[l], norm2_g[l], w_in[l], sg_ln_g[l], sg_ln_b[l],
              sg_w[l], sg_b[l], w_pa[l], w_pb[l], w_o[l], peer_wq[l], peer_subkeys[l],
              peer_u[l], peer_v[l])
        hp, kv_p, _ = decoder_layer(hp, c_prompt, None, *lp)
        past = [(cache_k[g][l], cache_v[g][l]) for g in range(N_GROUPS)]
        hs, kv_s, v_rows = decoder_layer(hs, c_sample, past, *lp)
        for g in range(N_GROUPS):
            kp[g].append(kv_p[g][0])
            vp[g].append(kv_p[g][1])
            ks_[g].append(kv_s[g][0])
            vs_[g].append(kv_s[g][1])
        sgv.append(v_rows)
    y_prompt = rmsnorm(hp, final_g)
    y_sample = rmsnorm(hs, final_g)
    return (y_prompt, y_sample,
            jnp.stack(kp[0]), jnp.stack(vp[0]), jnp.stack(ks_[0]), jnp.stack(vs_[0]),
            jnp.stack(kp[1]), jnp.stack(vp[1]), jnp.stack(ks_[1]), jnp.stack(vs_[1]),
            jnp.stack(kp[2]), jnp.stack(vp[2]), jnp.stack(ks_[2]), jnp.stack(vs_[2]),
            jnp.stack(sgv))
```

```python
import functools
import math

import jax
import jax.numpy as jnp
from jax import lax
from jax.experimental import pallas as pl
from jax.experimental.pallas import tpu as pltpu

F32 = jnp.float32
BF16 = jnp.bfloat16
EPS = 1e-6

VMEM_LIMIT_BYTES = 58 * 1024 * 1024
LANES = 128

HEAD_DIM = 128
N_GROUPS = 3
HEADS_PER_GROUP = 8
N_HEADS_ATT = N_GROUPS * HEADS_PER_GROUP
WINDOWS = (128, 512, 2048)
DILATIONS = (1, 4, 16)
ATT_SPAN = 128
CHUNK = 128
SG_GROUPS = 8
PEER_HEADS = 8
PEER_NKEYS = 128
PEER_TOPK = 16
N_MOD = 6

NT_DIMS = (((1,), (1,)), ((), ()))


def _params(*sem):
    return pltpu.CompilerParams(dimension_semantics=sem, vmem_limit_bytes=VMEM_LIMIT_BYTES)


def _gelu(x):
    return x * (lax.erf(x * (1.0 / math.sqrt(2.0))) + 1.0) * 0.5


class _Rows:
    def __init__(self, shared, rows_per_batch):
        self.shared = shared
        self.rows_per_batch = rows_per_batch

    def spec(self, tm, tn, col_block0=0):
        if self.shared:
            tpb = self.rows_per_batch // tm
            return pl.BlockSpec((None, 1, tn), lambda i, j: (i // tpb, 0, j + col_block0))
        return pl.BlockSpec((tm, tn), lambda i, j: (i, j + col_block0))


def _mod_kernel(c_ref, w_ref, b_ref, o_ref):
    c = c_ref[...]
    a = (c * jax.nn.sigmoid(c)).astype(BF16)
    o_ref[...] = jnp.dot(a, w_ref[...].astype(BF16), preferred_element_type=F32) + b_ref[...]


def _modulation(c, w, b, tn=1024):
    n, d = c.shape
    cols = w.shape[1]
    return pl.pallas_call(
        _mod_kernel,
        grid=(cols // tn,),
        in_specs=[pl.BlockSpec((n, d), lambda j: (0, 0)),
                  pl.BlockSpec((d, tn), lambda j: (0, j)),
                  pl.BlockSpec((1, tn), lambda j: (0, j))],
        out_specs=pl.BlockSpec((n, tn), lambda j: (0, j)),
        out_shape=jax.ShapeDtypeStruct((n, cols), F32),
        compiler_params=_params("parallel"),
        name="modulation",
    )(c, w, b.reshape(1, cols))


def _normmod_kernel(x_ref, g_ref, sc_ref, sh_ref, o_ref):
    x = x_ref[...]
    y = x * lax.rsqrt(jnp.mean(x * x, axis=-1, keepdims=True) + EPS)
    o_ref[...] = ((y * g_ref[...]) * (1.0 + sc_ref[...]) + sh_ref[...]).astype(o_ref.dtype)


def _normmod(x, g, mod, sc_idx, sh_idx, rows, tm):
    m, d = x.shape
    return pl.pallas_call(
        _normmod_kernel,
        grid=(m // tm, 1),
        in_specs=[pl.BlockSpec((tm, d), lambda i, j: (i, 0)),
                  pl.BlockSpec((1, d), lambda i, j: (0, 0)),
                  rows.spec(tm, d, sc_idx),
                  rows.spec(tm, d, sh_idx)],
        out_specs=pl.BlockSpec((tm, d), lambda i, j: (i, 0)),
        out_shape=jax.ShapeDtypeStruct((m, d), BF16),
        compiler_params=_params("parallel", "arbitrary"),
        name="normmod",
    )(x, g.reshape(1, d), mod, mod)


def _mm_kernel(a_ref, w_ref, o_ref):
    o_ref[...] = jnp.dot(a_ref[...], w_ref[...].astype(BF16), preferred_element_type=F32).astype(o_ref.dtype)


def _matmul(a, w, tm, tn, name):
    m, k = a.shape
    n = w.shape[1]
    return pl.pallas_call(
        _mm_kernel,
        grid=(m // tm, n // tn),
        in_specs=[pl.BlockSpec((tm, k), lambda i, j: (i, 0)),
                  pl.BlockSpec((k, tn), lambda i, j: (0, j))],
        out_specs=pl.BlockSpec((tm, tn), lambda i, j: (i, j)),
        out_shape=jax.ShapeDtypeStruct((m, n), F32),
        compiler_params=_params("parallel", "arbitrary"),
        name=name,
    )(a, w)


def _merge_kernel(att_ref, sg_ref, ga_ref, gb_ref, wa_ref, wb_ref, o_ref):
    pa = jnp.dot(att_ref[...], wa_ref[...].astype(BF16), preferred_element_type=F32)
    pb = jnp.dot(sg_ref[...], wb_ref[...].astype(BF16), preferred_element_type=F32)
    o_ref[...] = (jax.nn.sigmoid(ga_ref[...]) * pa + jax.nn.sigmoid(gb_ref[...]) * pb).astype(o_ref.dtype)


def _merge(att, sg, proj, ga_col, gb_col, w_pa, w_pb, tm, tn):
    m, ka = att.shape
    kb = sg.shape[1]
    n = w_pa.shape[1]
    ga0, gb0 = ga_col // tn, gb_col // tn
    return pl.pallas_call(
        _merge_kernel,
        grid=(m // tm, n // tn),
        in_specs=[pl.BlockSpec((tm, ka), lambda i, j: (i, 0)),
                  pl.BlockSpec((tm, kb), lambda i, j: (i, 0)),
                  pl.BlockSpec((tm, tn), lambda i, j: (i, j + ga0)),
                  pl.BlockSpec((tm, tn), lambda i, j: (i, j + gb0)),
                  pl.BlockSpec((ka, tn), lambda i, j: (0, j)),
                  pl.BlockSpec((kb, tn), lambda i, j: (0, j))],
        out_specs=pl.BlockSpec((tm, tn), lambda i, j: (i, j)),
        out_shape=jax.ShapeDtypeStruct((m, n), BF16),
        compiler_params=_params("parallel", "arbitrary"),
        name="merge",
    )(att, sg, proj, proj, w_pa, w_pb)


def _outproj_kernel(a_ref, w_ref, x_ref, gt_ref, o_ref):
    acc = jnp.dot(a_ref[...], w_ref[...].astype(BF16), preferred_element_type=F32)
    o_ref[...] = x_ref[...] + gt_ref[...] * acc


def _outproj(a, w, x, mod, gt_idx, rows, tm, tn):
    m, k = a.shape
    n = w.shape[1]
    return pl.pallas_call(
        _outproj_kernel,
        grid=(m // tm, n // tn),
        in_specs=[pl.BlockSpec((tm, k), lambda i, j: (i, 0)),
                  pl.BlockSpec((k, tn), lambda i, j: (0, j)),
                  pl.BlockSpec((tm, tn), lambda i, j: (i, j)),
                  rows.spec(tm, tn, gt_idx * (n // tn))],
        out_specs=pl.BlockSpec((tm, tn), lambda i, j: (i, j)),
        out_shape=jax.ShapeDtypeStruct((m, n), F32),
        compiler_params=_params("parallel", "arbitrary"),
        name="outproj",
    )(a, w, x, mod)


def _softmax_parts(s):
    m = jnp.max(s, axis=-1, keepdims=True)
    p = jnp.exp(s - m)
    return m, p, jnp.sum(p, axis=-1, keepdims=True)


def _combine_groups(outs, lses):
    top = functools.reduce(jnp.maximum, lses)
    es = [jnp.exp(l - top) for l in lses]
    tot = functools.reduce(jnp.add, es)
    return functools.reduce(jnp.add, [(e / tot) * o for e, o in zip(es, outs)])


def _attn_prompt_kernel(sl_ref, *refs, seq):
    q_refs, k_refs, v_refs, o_ref = refs[0:9:3], refs[1:9:3], refs[2:9:3], refs[9]
    h = pl.program_id(1)
    scale = HEAD_DIM ** -0.5
    qb_rows = ATT_SPAN

    def block(qb, carry):
        t0 = pl.multiple_of(qb * qb_rows, qb_rows)
        outs, lses = [], []
        for g in range(N_GROUPS):
            size = min(WINDOWS[g] + qb_rows, seq)
            lo = pl.multiple_of(jnp.maximum(t0 + qb_rows - size, 0), qb_rows)
            q = q_refs[g][pl.ds(t0, qb_rows), :].astype(BF16)
            k = k_refs[g][pl.ds(lo, size), :].astype(BF16)
            v = v_refs[g][pl.ds(lo, size), :].astype(BF16)
            s = lax.dot_general(q, k, NT_DIMS, preferred_element_type=F32) * scale
            delta = (t0 - lo) + (lax.broadcasted_iota(jnp.int32, (qb_rows, size), 0)
                                 - lax.broadcasted_iota(jnp.int32, (qb_rows, size), 1))
            valid = (delta >= 0) & (delta <= WINDOWS[g]) & ((delta & (DILATIONS[g] - 1)) == 0)
            s = jnp.where(valid, s - sl_ref[g, h] * delta.astype(F32), -jnp.inf)
            m, p, den = _softmax_parts(s)
            outs.append(jnp.dot((p / den).astype(BF16), v, preferred_element_type=F32))
            lses.append(m + jnp.log(den))
        o_ref[pl.ds(t0, qb_rows), :] = _combine_groups(outs, lses).astype(o_ref.dtype)
        return carry

    lax.fori_loop(0, seq // qb_rows, block, 0)


def _attn_prompt(qkv, slopes, batch, seq):
    ncols = qkv.shape[1]
    qkv3 = qkv.reshape(batch, seq, ncols)
    in_specs = [pl.BlockSpec(memory_space=pltpu.SMEM)]
    for g in range(N_GROUPS):
        for part in range(3):
            cb = part * N_HEADS_ATT + g * HEADS_PER_GROUP
            in_specs.append(pl.BlockSpec((None, seq, HEAD_DIM), lambda b, h, cb=cb: (b, 0, cb + h)))
    out = pl.pallas_call(
        functools.partial(_attn_prompt_kernel, seq=seq),
        grid=(batch, HEADS_PER_GROUP),
        in_specs=in_specs,
        out_specs=pl.BlockSpec((None, seq, HEAD_DIM), lambda b, h: (b, 0, h)),
        out_shape=jax.ShapeDtypeStruct((batch, seq, HEADS_PER_GROUP * HEAD_DIM), BF16),
        compiler_params=_params("parallel", "arbitrary"),
        name="attn_prompt",
    )(slopes, *([qkv3] * 9))
    return out.reshape(batch * seq, HEADS_PER_GROUP * HEAD_DIM)


def _attn_sample_kernel(sl_ref, qkv_ref, k0, v0, k1, v1, k2, v2, o_ref, *, dec_seq):
    scale = HEAD_DIM ** -0.5
    att_w = N_HEADS_ATT * HEAD_DIM
    caches = ((k0, v0), (k1, v1), (k2, v2))
    t_new = lax.broadcasted_iota(jnp.int32, (dec_seq, dec_seq), 0)
    c_new = lax.broadcasted_iota(jnp.int32, (dec_seq, dec_seq), 1)
    for h in range(HEADS_PER_GROUP):
        outs, lses = [], []
        for g in range(N_GROUPS):
            dil, win = DILATIONS[g], WINDOWS[g]
            col = (g * HEADS_PER_GROUP + h) * HEAD_DIM
            q = qkv_ref[:, col:col + HEAD_DIM].astype(BF16)
            kn = qkv_ref[:, att_w + col:att_w + col + HEAD_DIM].astype(BF16)
            vn = qkv_ref[:, 2 * att_w + col:2 * att_w + col + HEAD_DIM].astype(BF16)
            kc_ref, vc_ref = caches[g]
            hs = slice(h * HEAD_DIM, (h + 1) * HEAD_DIM)
            if g == 2:
                kc = kc_ref[:, :, hs].reshape(-1, HEAD_DIM).astype(BF16)
                vc = vc_ref[:, :, hs].reshape(-1, HEAD_DIM).astype(BF16)
            else:
                kc = kc_ref[:, hs].astype(BF16)
                vc = vc_ref[:, hs].astype(BF16)
            rows = kc.shape[0]
            t_c = lax.broadcasted_iota(jnp.int32, (dec_seq, rows), 0)
            c_c = lax.broadcasted_iota(jnp.int32, (dec_seq, rows), 1)
            if g == 2:
                e_c = ((c_c >> 3) << 4) + (c_c & 7)
            else:
                e_c = c_c
            d_c = win + t_c - e_c
            ok_c = ((d_c & (dil - 1)) == 0) & (d_c <= ATT_SPAN * dil)
            d_n = t_new - c_new
            ok_n = (d_n >= 0) & ((d_n & (dil - 1)) == 0)
            slope = sl_ref[g, h]
            s_c = lax.dot_general(q, kc, NT_DIMS, preferred_element_type=F32) * scale
            s_n = lax.dot_general(q, kn, NT_DIMS, preferred_element_type=F32) * scale
            s_c = jnp.where(ok_c, s_c - slope * d_c.astype(F32), -jnp.inf)
            s_n = jnp.where(ok_n, s_n - slope * d_n.astype(F32), -jnp.inf)
            m = jnp.maximum(jnp.max(s_c, axis=-1, keepdims=True), jnp.max(s_n, axis=-1, keepdims=True))
            p_c = jnp.exp(s_c - m)
            p_n = jnp.exp(s_n - m)
            den = jnp.sum(p_c, axis=-1, keepdims=True) + jnp.sum(p_n, axis=-1, keepdims=True)
            o = (jnp.dot((p_c / den).astype(BF16), vc, preferred_element_type=F32)
                 + jnp.dot((p_n / den).astype(BF16), vn, preferred_element_type=F32))
            outs.append(o)
            lses.append(m + jnp.log(den))
        o_ref[:, h * HEAD_DIM:(h + 1) * HEAD_DIM] = _combine_groups(outs, lses)


def _attn_sample(qkv, slopes, caches, dec_batch, dec_seq):
    att_w = N_HEADS_ATT * HEAD_DIM
    hw = HEADS_PER_GROUP * HEAD_DIM
    ins, specs = [], []
    for g in range(N_GROUPS):
        for c in caches[g]:
            win = c.shape[1]
            if g == 2:
                dil = DILATIONS[g]
                ins.append(c.reshape(dec_batch, win // dil, dil, hw))
                specs.append(pl.BlockSpec((None, win // dil, dec_seq, hw), lambda b: (b, 0, 0, 0)))
            else:
                ins.append(c.reshape(dec_batch, win, hw))
                specs.append(pl.BlockSpec((None, win, hw), lambda b: (b, 0, 0)))
    return pl.pallas_call(
        functools.partial(_attn_sample_kernel, dec_seq=dec_seq),
        grid=(dec_batch,),
        in_specs=[pl.BlockSpec(memory_space=pltpu.SMEM),
                  pl.BlockSpec((dec_seq, 3 * att_w), lambda b: (b, 0))] + specs,
        out_specs=pl.BlockSpec((dec_seq, hw), lambda b: (b, 0)),
        out_shape=jax.ShapeDtypeStruct((dec_batch * dec_seq, hw), F32),
        compiler_params=_params("parallel"),
        name="attn_sample",
    )(slopes, qkv, *ins)


def _sgu_kernel(*refs, block_rows, parts, want_vn):
    su_refs, sv_refs = refs[:parts], refs[parts:2 * parts]
    lg_ref, lb_ref, w_ref, b_ref, sg_ref = refs[2 * parts:2 * parts + 5]

    def whole(rs):
        return rs[0][...] if parts == 1 else jnp.concatenate([r[...] for r in rs], axis=1)

    u = _gelu(whole(su_refs))
    v = _gelu(whole(sv_refs))
    mu = jnp.mean(v, axis=-1, keepdims=True)
    vc = v - mu
    vn = vc * lax.rsqrt(jnp.mean(vc * vc, axis=-1, keepdims=True) + EPS) * lg_ref[...] + lb_ref[...]
    if want_vn:
        refs[2 * parts + 5][...] = vn
    r = lax.broadcasted_iota(jnp.int32, (CHUNK, CHUNK), 0)
    c = lax.broadcasted_iota(jnp.int32, (CHUNK, CHUNK), 1)
    keep = (c <= r) & ((r // block_rows) == (c // block_rows))
    ch = vn.shape[1] // SG_GROUPS
    for g in range(SG_GROUPS):
        w = jnp.where(keep, w_ref[g], 0.0).astype(BF16)
        mixed = jnp.dot(w, vn[:, g * ch:(g + 1) * ch].astype(BF16), preferred_element_type=F32) + b_ref[:, g:g + 1]
        sg_ref[:, g * ch:(g + 1) * ch] = (u[:, g * ch:(g + 1) * ch] * mixed).astype(sg_ref.dtype)


def _sgu(proj, su_col, sv_col, d_sg, ln_g, ln_b, w, bias_t, block_rows, want_vn):
    m = proj.shape[0]
    out_shape = [jax.ShapeDtypeStruct((m, d_sg), BF16)]
    out_specs = [pl.BlockSpec((CHUNK, d_sg), lambda i: (i, 0))]
    if want_vn:
        out_shape.append(jax.ShapeDtypeStruct((m, d_sg), F32))
        out_specs.append(pl.BlockSpec((CHUNK, d_sg), lambda i: (i, 0)))
    unit = math.gcd(math.gcd(su_col, sv_col), d_sg)
    parts = d_sg // unit
    col_specs = [pl.BlockSpec((CHUNK, unit), lambda i, cb=c0 // unit + k: (i, cb))
                 for c0 in (su_col, sv_col) for k in range(parts)]
    return pl.pallas_call(
        functools.partial(_sgu_kernel, block_rows=block_rows, parts=parts, want_vn=want_vn),
        grid=(m // CHUNK,),
        in_specs=col_specs + [pl.BlockSpec((1, d_sg), lambda i: (0, 0)),
                              pl.BlockSpec((1, d_sg), lambda i: (0, 0)),
                              pl.BlockSpec((SG_GROUPS, CHUNK, CHUNK), lambda i: (0, 0, 0)),
                              pl.BlockSpec((CHUNK, SG_GROUPS), lambda i: (0, 0))],
        out_specs=out_specs,
        out_shape=out_shape,
        compiler_params=_params("parallel"),
        name="sgu",
    )(*([proj] * (2 * parts)), ln_g.reshape(1, d_sg), ln_b.reshape(1, d_sg), w, bias_t)


def _take_top(s, key_id, n_take):
    rows, cols = s.shape
    slot = lax.broadcasted_iota(jnp.int32, (n_take, cols), 0)
    rank = jnp.full((rows, cols), float(n_take), F32)
    top = jnp.zeros((n_take, cols), F32)
    for r in range(n_take):
        m = jnp.max(s, axis=0, keepdims=True)
        first = jnp.min(jnp.where(s == m, key_id, float(rows)), axis=0, keepdims=True)
        sel = key_id == first
        rank = jnp.where(sel, float(r), rank)
        s = jnp.where(sel, -jnp.inf, s)
        top = jnp.where(slot == r, m, top)
    return top, rank


def _peer_route_kernel(q_ref, sk_ref, rk_ref, e1_ref, lk_ref, e0_ref):
    tt = q_ref.shape[0]
    half = q_ref.shape[1] // 2
    nk = sk_ref.shape[1]
    key_id = lax.broadcasted_iota(jnp.int32, (nk, tt), 0).astype(F32)
    scores, tops, ranks = [], [], []
    for p in range(2):
        s = lax.dot_general(sk_ref[p].astype(BF16), q_ref[:, p * half:(p + 1) * half].astype(BF16),
                            NT_DIMS, preferred_element_type=F32)
        top, rank = _take_top(s, key_id, PEER_TOPK)
        scores.append(s)
        tops.append(top)
        ranks.append(rank)
    top0, top1 = tops

    sub = lax.broadcasted_iota(jnp.int32, (8, tt), 0)
    subf = sub.astype(F32)
    pieces, flat = [], []
    for a in range(8):
        n_b = PEER_TOPK // (a + 1)
        for b0 in range(0, n_b, 8):
            c = top0[a:a + 1, :] + top1[b0:b0 + 8, :]
            pieces.append(jnp.where(sub < n_b - b0, c, -jnp.inf))
            flat.append(subf + float(a * PEER_TOPK + b0))
    pieces.append(top0[8:16, :] + top1[0:1, :])
    flat.append((subf + 8.0) * float(PEER_TOPK))
    cand = list(pieces)
    taken = [jnp.zeros((8, tt), F32) for _ in pieces]
    big = float(PEER_TOPK * PEER_TOPK)
    for _ in range(PEER_TOPK):
        m = jnp.max(functools.reduce(jnp.maximum, cand), axis=0, keepdims=True)
        first = jnp.min(functools.reduce(jnp.minimum, [jnp.where(c == m, f, big) for c, f in zip(cand, flat)]),
                        axis=0, keepdims=True)
        for i, f in enumerate(flat):
            sel = f == first
            cand[i] = jnp.where(sel, -jnp.inf, cand[i])
            taken[i] = jnp.where(sel, 1.0, taken[i])
    best = top0[0:1, :] + top1[0:1, :]
    z = functools.reduce(jnp.add, [jnp.sum(jnp.where(t > 0.0, jnp.exp(c - best), 0.0), axis=0, keepdims=True)
                                   for t, c in zip(taken, pieces)])
    counts = [jnp.sum(taken[0] + taken[1], axis=0, keepdims=True)]
    counts += [jnp.sum(taken[a + 1], axis=0, keepdims=True) for a in range(1, 8)]
    counts += [taken[-1][a:a + 1, :] for a in range(8)]
    lk = jnp.zeros((nk, tt), F32)
    for a in range(PEER_TOPK):
        lk = jnp.where(ranks[0] == float(a), counts[a], lk)
    rk_ref[...] = ranks[1]
    e1_ref[...] = jnp.exp(scores[1] - top1[0:1, :])
    lk_ref[...] = lk
    e0_ref[...] = jnp.exp(scores[0] - top0[0:1, :]) / z


def _peer_route(qp, sub_keys, tt):
    m = qp.shape[0]
    nh, _, nk, half = sub_keys.shape
    shp = jax.ShapeDtypeStruct((nh, nk, m), F32)
    ospec = pl.BlockSpec((None, nk, tt), lambda i, h: (h, 0, i))
    return pl.pallas_call(
        _peer_route_kernel,
        grid=(m // tt, nh),
        in_specs=[pl.BlockSpec((tt, 2 * half), lambda i, h: (i, h)),
                  pl.BlockSpec((None, 2, nk, half), lambda i, h: (h, 0, 0, 0))],
        out_specs=[ospec] * 4,
        out_shape=[shp] * 4,
        compiler_params=_params("parallel", "arbitrary"),
        name="peer_route",
    )(qp, sub_keys)


def _peer_dense_kernel(h_ref, u_ref, v_ref, rk_ref, e1_ref, lk_ref, e0_ref, o_ref):
    c = pl.program_id(1)
    ec = u_ref.shape[0]
    nk = rk_ref.shape[1]
    act = lax.dot_general(h_ref[...], u_ref[...], NT_DIMS, preferred_element_type=F32)
    parts = []
    for ii in range(ec // nk):
        i = c * (ec // nk) + ii
        w = None
        for hd in range(rk_ref.shape[0]):
            take = rk_ref[hd] < lk_ref[hd, pl.ds(i, 1), :]
            term = jnp.where(take, e1_ref[hd], 0.0) * e0_ref[hd, pl.ds(i, 1), :]
            w = term if w is None else w + term
        parts.append(w)
    gate_t = parts[0] if len(parts) == 1 else jnp.concatenate(parts, axis=0)
    gated = (gate_t.T * _gelu(act)).astype(BF16)
    contrib = jnp.dot(gated, v_ref[...], preferred_element_type=F32)

    @pl.when(c == 0)
    def _():
        o_ref[...] = contrib

    @pl.when(c != 0)
    def _():
        o_ref[...] += contrib


def _peer_dense(h2, u_bf, v_bf, route, tt, ec):
    m, d = h2.shape
    n_exp = u_bf.shape[0]
    nh, nk, _ = route[0].shape
    rspec = pl.BlockSpec((nh, nk, tt), lambda i, c: (0, 0, i), pipeline_mode=pl.Buffered(1))
    return pl.pallas_call(
        _peer_dense_kernel,
        grid=(m // tt, n_exp // ec),
        in_specs=[pl.BlockSpec((tt, d), lambda i, c: (i, 0), pipeline_mode=pl.Buffered(1)),
                  pl.BlockSpec((ec, d), lambda i, c: (c, 0)),
                  pl.BlockSpec((ec, d), lambda i, c: (c, 0)),
                  rspec, rspec, rspec, rspec],
        out_specs=pl.BlockSpec((tt, d), lambda i, c: (i, 0)),
        out_shape=jax.ShapeDtypeStruct((m, d), F32),
        compiler_params=_params("parallel", "arbitrary"),
        name="peer_dense",
    )(h2, u_bf, v_bf, *route)


def _final_kernel(x_ref, p_ref, gt_ref, g_ref, o_ref):
    x = x_ref[...] + gt_ref[...] * p_ref[...]
    o_ref[...] = x * lax.rsqrt(jnp.mean(x * x, axis=-1, keepdims=True) + EPS) * g_ref[...]


def _final(x, peer, mod, gt_idx, g, rows, tm):
    m, d = x.shape
    return pl.pallas_call(
        _final_kernel,
        grid=(m // tm, 1),
        in_specs=[pl.BlockSpec((tm, d), lambda i, j: (i, 0)),
                  pl.BlockSpec((tm, d), lambda i, j: (i, 0)),
                  rows.spec(tm, d, gt_idx),
                  pl.BlockSpec((1, d), lambda i, j: (0, 0))],
        out_specs=pl.BlockSpec((tm, d), lambda i, j: (i, 0)),
        out_shape=jax.ShapeDtypeStruct((m, d), F32),
        compiler_params=_params("parallel", "arbitrary"),
        name="final",
    )(x, peer, mod, g.reshape(1, d))


def _tile(n, want):
    t = min(n, want)
    assert n % t == 0, (n, want)
    return t


def _layer(x, mod, rows, attend, sgu_block_rows, want_vn, p):
    m, d = x.shape
    d_sg = d // 2
    att_w = N_HEADS_ATT * HEAD_DIM
    tm = _tile(m, 1024)
    tm_row = _tile(m, 256)
    h1 = _normmod(x, p["norm1_g"], mod, 1, 0, rows, tm_row)
    proj = _matmul(h1, p["w_in"], tm, 512, "inproj")
    att = attend(proj)
    sg_out = _sgu(proj, 3 * att_w, 3 * att_w + d_sg, d_sg, p["sg_ln_g"], p["sg_ln_b"],
                  p["sg_w_eff"][sgu_block_rows], p["sg_b_eff"][sgu_block_rows], sgu_block_rows, want_vn)
    sg = sg_out[0]
    merged = _merge(att.astype(BF16), sg, proj, 3 * att_w + 2 * d_sg, 3 * att_w + 2 * d_sg + d,
                    p["w_pa"], p["w_pb"], tm, 512)
    x1 = _outproj(merged, p["w_o"], x, mod, 2, rows, tm, 512)
    h2 = _normmod(x1, p["norm2_g"], mod, 4, 3, rows, tm_row)
    qp = _matmul(h2, p["peer_wq"], tm, 512, "peer_q")
    tt = _tile(m, 512)
    route = _peer_route(qp, p["peer_subkeys"], tt)
    peer = _peer_dense(h2, p["peer_u_bf"], p["peer_v_bf"], route, tt, _tile(p["peer_u_bf"].shape[0], 512))
    y = _final(x1, peer, mod, 5, p["final_g"], rows, tm_row)
    return y, proj, (sg_out[1] if want_vn else None)


def kernel(x_prompt, x_sample, cache_k0, cache_v0, cache_k1, cache_v1, cache_k2, cache_v2, c_prompt, c_sample,
           ada_w, ada_b, norm1_g, norm2_g, w_in, sg_ln_g, sg_ln_b, sg_w, sg_b, w_pa, w_pb, w_o, peer_wq,
           peer_subkeys, peer_u, peer_v, final_g):
    batch, seq, d = x_prompt.shape
    dec_batch, dec_seq, _ = x_sample.shape
    depth = ada_w.shape[0]
    assert depth == 1, "single-layer step"
    att_w = N_HEADS_ATT * HEAD_DIM
    hw = HEADS_PER_GROUP * HEAD_DIM

    slopes = jnp.exp2(-8.0 * jnp.arange(1, N_HEADS_ATT + 1, dtype=F32) / N_HEADS_ATT).reshape(N_GROUPS, HEADS_PER_GROUP)
    sg_wl, sg_bl = sg_w[0], sg_b[0]
    reps = CHUNK // dec_seq
    p = dict(norm1_g=norm1_g[0], norm2_g=norm2_g[0], w_in=w_in[0], sg_ln_g=sg_ln_g[0], sg_ln_b=sg_ln_b[0],
             w_pa=w_pa[0], w_pb=w_pb[0], w_o=w_o[0], peer_wq=peer_wq[0], peer_subkeys=peer_subkeys[0],
             peer_u_bf=peer_u[0].astype(BF16), peer_v_bf=peer_v[0].astype(BF16), final_g=final_g,
             sg_w_eff={CHUNK: sg_wl, dec_seq: jnp.tile(sg_wl[:, :dec_seq, :dec_seq], (1, reps, reps))},
             sg_b_eff={CHUNK: sg_bl.T, dec_seq: jnp.tile(sg_bl.T[:dec_seq], (reps, 1))})

    c_all = jnp.concatenate([c_prompt, c_sample], axis=0)
    mod = _modulation(c_all, ada_w[0], ada_b[0])
    mod_p = mod[:batch].reshape(batch, 1, N_MOD * d)
    mod_s = jnp.repeat(mod[batch:], dec_seq, axis=0)

    y_p, proj_p, _ = _layer(
        x_prompt.reshape(batch * seq, d), mod_p, _Rows(True, seq),
        lambda proj: _attn_prompt(proj, slopes, batch, seq), CHUNK, False, p)

    caches = [(cache_k0[0], cache_v0[0]), (cache_k1[0], cache_v1[0]), (cache_k2[0], cache_v2[0])]
    y_s, proj_s, vn_s = _layer(
        x_sample.reshape(dec_batch * dec_seq, d), mod_s, _Rows(False, dec_seq),
        lambda proj: _attn_sample(proj, slopes, caches, dec_batch, dec_seq), dec_seq, True, p)

    outs = [y_p.reshape(batch, seq, d), y_s.reshape(dec_batch, dec_seq, d)]
    kp = proj_p.reshape(batch, seq, -1)
    ks = proj_s.reshape(dec_batch, dec_seq, -1)
    for g in range(N_GROUPS):
        keep = min(WINDOWS[g], seq)
        for src, rows_from in ((kp, seq - keep), (ks, 0)):
            for part in (1, 2):
                c0 = part * att_w + g * hw
                blk = src[:, rows_from:, c0:c0 + hw]
                outs.append(blk.reshape(1, blk.shape[0], blk.shape[1], HEADS_PER_GROUP, HEAD_DIM))
    outs.append(vn_s.reshape(1, dec_batch, dec_seq, d // 2))
    return tuple(outs)
```

```python
import functools
import math

import jax
import jax.numpy as jnp
from jax import lax
from jax.experimental import pallas as pl
from jax.experimental.pallas import tpu as pltpu

F32 = jnp.float32
BF16 = jnp.bfloat16
EPS = 1e-6

VMEM_LIMIT_BYTES = 58 * 1024 * 1024
LANES = 128

HEAD_DIM = 128
N_GROUPS = 3
HEADS_PER_GROUP = 8
N_HEADS_ATT = N_GROUPS * HEADS_PER_GROUP
WINDOWS = (128, 512, 2048)
DILATIONS = (1, 4, 16)
ATT_SPAN = 128
CHUNK = 128
SG_GROUPS = 8
PEER_HEADS = 8
PEER_NKEYS = 128
PEER_TOPK = 16
N_MOD = 6

NT_DIMS = (((1,), (1,)), ((), ()))


def _params(*sem):
    return pltpu.CompilerParams(dimension_semantics=sem, vmem_limit_bytes=VMEM_LIMIT_BYTES)


def _gelu(x):
    return x * (lax.erf(x * (1.0 / math.sqrt(2.0))) + 1.0) * 0.5


class _Rows:
    def __init__(self, shared, rows_per_batch):
        self.shared = shared
        self.rows_per_batch = rows_per_batch

    def spec(self, tm, tn, col_block0=0):
        if self.shared:
            tpb = self.rows_per_batch // tm
            return pl.BlockSpec((None, 1, tn), lambda i, j: (i // tpb, 0, j + col_block0))
        return pl.BlockSpec((tm, tn), lambda i, j: (i, j + col_block0))


def _mod_kernel(c_ref, w_ref, b_ref, o_ref):
    c = c_ref[...]
    a = (c * jax.nn.sigmoid(c)).astype(BF16)
    o_ref[...] = jnp.dot(a, w_ref[...].astype(BF16), preferred_element_type=F32) + b_ref[...]


def _modulation(c, w, b, tn=1024):
    n, d = c.shape
    cols = w.shape[1]
    return pl.pallas_call(
        _mod_kernel,
        grid=(cols // tn,),
        in_specs=[pl.BlockSpec((n, d), lambda j: (0, 0)),
                  pl.BlockSpec((d, tn), lambda j: (0, j)),
                  pl.BlockSpec((1, tn), lambda j: (0, j))],
        out_specs=pl.BlockSpec((n, tn), lambda j: (0, j)),
        out_shape=jax.ShapeDtypeStruct((n, cols), F32),
        compiler_params=_params("parallel"),
        name="modulation",
    )(c, w, b.reshape(1, cols))


def _normmod_kernel(x_ref, g_ref, sc_ref, sh_ref, o_ref):
    x = x_ref[...]
    y = x * lax.rsqrt(jnp.mean(x * x, axis=-1, keepdims=True) + EPS)
    o_ref[...] = ((y * g_ref[...]) * (1.0 + sc_ref[...]) + sh_ref[...]).astype(o_ref.dtype)


def _normmod(x, g, mod, sc_idx, sh_idx, rows, tm):
    m, d = x.shape
    return pl.pallas_call(
        _normmod_kernel,
        grid=(m // tm, 1),
        in_specs=[pl.BlockSpec((tm, d), lambda i, j: (i, 0)),
                  pl.BlockSpec((1, d), lambda i, j: (0, 0)),
                  rows.spec(tm, d, sc_idx),
                  rows.spec(tm, d, sh_idx)],
        out_specs=pl.BlockSpec((tm, d), lambda i, j: (i, 0)),
        out_shape=jax.ShapeDtypeStruct((m, d), BF16),
        compiler_params=_params("parallel", "arbitrary"),
        name="normmod",
    )(x, g.reshape(1, d), mod, mod)


def _mm_kernel(a_ref, w_ref, o_ref):
    o_ref[...] = jnp.dot(a_ref[...], w_ref[...].astype(BF16), preferred_element_type=F32).astype(o_ref.dtype)


def _matmul(a, w, tm, tn, name):
    m, k = a.shape
    n = w.shape[1]
    return pl.pallas_call(
        _mm_kernel,
        grid=(m // tm, n // tn),
        in_specs=[pl.BlockSpec((tm, k), lambda i, j: (i, 0)),
                  pl.BlockSpec((k, tn), lambda i, j: (0, j))],
        out_specs=pl.BlockSpec((tm, tn), lambda i, j: (i, j)),
        out_shape=jax.ShapeDtypeStruct((m, n), F32),
        compiler_params=_params("parallel", "arbitrary"),
        name=name,
    )(a, w)


def _merge_kernel(att_ref, sg_ref, ga_ref, gb_ref, wa_ref, wb_ref, o_ref):
    pa = jnp.dot(att_ref[...], wa_ref[...].astype(BF16), preferred_element_type=F32)
    pb = jnp.dot(sg_ref[...], wb_ref[...].astype(BF16), preferred_element_type=F32)
    o_ref[...] = (jax.nn.sigmoid(ga_ref[...]) * pa + jax.nn.sigmoid(gb_ref[...]) * pb).astype(o_ref.dtype)


def _merge(att, sg, proj, ga_col, gb_col, w_pa, w_pb, tm, tn):
    m, ka = att.shape
    kb = sg.shape[1]
    n = w_pa.shape[1]
    ga0, gb0 = ga_col // tn, gb_col // tn
    return pl.pallas_call(
        _merge_kernel,
        grid=(m // tm, n // tn),
        in_specs=[pl.BlockSpec((tm, ka), lambda i, j: (i, 0)),
                  pl.BlockSpec((tm, kb), lambda i, j: (i, 0)),
                  pl.BlockSpec((tm, tn), lambda i, j: (i, j + ga0)),
                  pl.BlockSpec((tm, tn), lambda i, j: (i, j + gb0)),
                  pl.BlockSpec((ka, tn), lambda i, j: (0, j)),
                  pl.BlockSpec((kb, tn), lambda i, j: (0, j))],
        out_specs=pl.BlockSpec((tm, tn), lambda i, j: (i, j)),
        out_shape=jax.ShapeDtypeStruct((m, n), BF16),
        compiler_params=_params("parallel", "arbitrary"),
        name="merge",
    )(att, sg, proj, proj, w_pa, w_pb)


def _outproj_kernel(a_ref, w_ref, x_ref, gt_ref, o_ref):
    acc = jnp.dot(a_ref[...], w_ref[...].astype(BF16), preferred_element_type=F32)
    o_ref[...] = x_ref[...] + gt_ref[...] * acc


def _outproj(a, w, x, mod, gt_idx, rows, tm, tn):
    m, k = a.shape
    n = w.shape[1]
    return pl.pallas_call(
        _outproj_kernel,
        grid=(m // tm, n // tn),
        in_specs=[pl.BlockSpec((tm, k), lambda i, j: (i, 0)),
                  pl.BlockSpec((k, tn), lambda i, j: (0, j)),
                  pl.BlockSpec((tm, tn), lambda i, j: (i, j)),
                  rows.spec(tm, tn, gt_idx * (n // tn))],
        out_specs=pl.BlockSpec((tm, tn), lambda i, j: (i, j)),
        out_shape=jax.ShapeDtypeStruct((m, n), F32),
        compiler_params=_params("parallel", "arbitrary"),
        name="outproj",
    )(a, w, x, mod)


def _softmax_parts(s):
    m = jnp.max(s, axis=-1, keepdims=True)
    p = jnp.exp(s - m)
    return m, p, jnp.sum(p, axis=-1, keepdims=True)


def _combine_groups(outs, lses):
    top = functools.reduce(jnp.maximum, lses)
    es = [jnp.exp(l - top) for l in lses]
    tot = functools.reduce(jnp.add, es)
    return functools.reduce(jnp.add, [(e / tot) * o for e, o in zip(es, outs)])


def _attn_prompt_kernel(sl_ref, *refs, seq):
    q_refs, k_refs, v_refs = refs[0:9:3], refs[1:9:3], refs[2:9:3]
    o_ref, og_ref, lg_ref = refs[9], refs[10], refs[11]
    h = pl.program_id(1)
    scale = HEAD_DIM ** -0.5
    span = ATT_SPAN
    back1 = lax.broadcasted_iota(jnp.int32, (span, span), 0) - lax.broadcasted_iota(jnp.int32, (span, span), 1)
    back2 = (lax.broadcasted_iota(jnp.int32, (span, 2 * span), 0) + span
             - lax.broadcasted_iota(jnp.int32, (span, 2 * span), 1))
    for g in range(N_GROUPS):
        dil = DILATIONS[g]
        n_blocks = seq // dil // span
        step_bias = sl_ref[g, h] * float(dil)
        bias1 = jnp.where(back1 >= 0, -(step_bias * back1.astype(F32)), -jnp.inf)
        bias2 = jnp.where((back2 >= 0) & (back2 <= span), -(step_bias * back2.astype(F32)), -jnp.inf)
        for res in range(dil):
            k_prev = v_prev = None
            for blk in range(n_blocks):
                start = res + dil * span * blk
                rows = pl.ds(start, span) if dil == 1 else pl.ds(start, span, stride=dil)
                q = q_refs[g][rows, :].astype(BF16)
                k_cur = k_refs[g][rows, :].astype(BF16)
                v_cur = v_refs[g][rows, :].astype(BF16)
                if blk == 0:
                    k, v, bias = k_cur, v_cur, bias1
                else:
                    k = jnp.concatenate([k_prev, k_cur], axis=0)
                    v = jnp.concatenate([v_prev, v_cur], axis=0)
                    bias = bias2
                s = lax.dot_general(q, k, NT_DIMS, preferred_element_type=F32) * scale + bias
                m, p, den = _softmax_parts(s)
                og_ref[g, rows, :] = jnp.dot((p / den).astype(BF16), v, preferred_element_type=F32)
                lg_ref[g, rows, :] = jnp.broadcast_to(m + jnp.log(den), (span, HEAD_DIM))
                k_prev, v_prev = k_cur, v_cur

    def combine(i, carry):
        rows = pl.ds(pl.multiple_of(i * span, span), span)
        o_ref[rows, :] = _combine_groups([og_ref[g, rows, :] for g in range(N_GROUPS)],
                                         [lg_ref[g, rows, :] for g in range(N_GROUPS)]).astype(o_ref.dtype)
        return carry

    lax.fori_loop(0, seq // span, combine, 0)


def _attn_prompt(qkv, slopes, batch, seq):
    ncols = qkv.shape[1]
    assert seq % (max(DILATIONS) * ATT_SPAN) == 0, "every residue class needs whole query blocks"
    qkv3 = qkv.reshape(batch, seq, ncols)
    in_specs = [pl.BlockSpec(memory_space=pltpu.SMEM)]
    for g in range(N_GROUPS):
        for part in range(3):
            cb = part * N_HEADS_ATT + g * HEADS_PER_GROUP
            in_specs.append(pl.BlockSpec((None, seq, HEAD_DIM), lambda b, h, cb=cb: (b, 0, cb + h)))
    out = pl.pallas_call(
        functools.partial(_attn_prompt_kernel, seq=seq),
        grid=(batch, HEADS_PER_GROUP),
        in_specs=in_specs,
        out_specs=pl.BlockSpec((None, seq, HEAD_DIM), lambda b, h: (b, 0, h)),
        out_shape=jax.ShapeDtypeStruct((batch, seq, HEADS_PER_GROUP * HEAD_DIM), BF16),
        scratch_shapes=[pltpu.VMEM((N_GROUPS, seq, HEAD_DIM), F32),
                        pltpu.VMEM((N_GROUPS, seq, HEAD_DIM), F32)],
        compiler_params=_params("parallel", "arbitrary"),
        name="attn_prompt",
    )(slopes, *([qkv3] * 9))
    return out.reshape(batch * seq, HEADS_PER_GROUP * HEAD_DIM)


def _attn_sample_kernel(sl_ref, qkv_ref, k0, v0, k1, v1, k2, v2, o_ref, *, dec_seq):
    scale = HEAD_DIM ** -0.5
    att_w = N_HEADS_ATT * HEAD_DIM
    caches = ((k0, v0), (k1, v1), (k2, v2))
    t_new = lax.broadcasted_iota(jnp.int32, (dec_seq, dec_seq), 0)
    c_new = lax.broadcasted_iota(jnp.int32, (dec_seq, dec_seq), 1)
    for h in range(HEADS_PER_GROUP):
        outs, lses = [], []
        for g in range(N_GROUPS):
            dil, win = DILATIONS[g], WINDOWS[g]
            col = (g * HEADS_PER_GROUP + h) * HEAD_DIM
            q = qkv_ref[:, col:col + HEAD_DIM].astype(BF16)
            kn = qkv_ref[:, att_w + col:att_w + col + HEAD_DIM].astype(BF16)
            vn = qkv_ref[:, 2 * att_w + col:2 * att_w + col + HEAD_DIM].astype(BF16)
            kc_ref, vc_ref = caches[g]
            rows = math.prod(kc_ref.shape[:-2])
            head_rows = pl.ds(h, rows, stride=HEADS_PER_GROUP)
            kc = kc_ref.reshape(rows * HEADS_PER_GROUP, HEAD_DIM)[head_rows, :].astype(BF16)
            vc = vc_ref.reshape(rows * HEADS_PER_GROUP, HEAD_DIM)[head_rows, :].astype(BF16)
            t_c = lax.broadcasted_iota(jnp.int32, (dec_seq, rows), 0)
            c_c = lax.broadcasted_iota(jnp.int32, (dec_seq, rows), 1)
            if g == 2:
                e_c = ((c_c >> 3) << 4) + (c_c & 7)
            else:
                e_c = c_c
            d_c = win + t_c - e_c
            ok_c = ((d_c & (dil - 1)) == 0) & (d_c <= ATT_SPAN * dil)
            d_n = t_new - c_new
            ok_n = (d_n >= 0) & ((d_n & (dil - 1)) == 0)
            slope = sl_ref[g, h]
            s_c = lax.dot_general(q, kc, NT_DIMS, preferred_element_type=F32) * scale
            s_n = lax.dot_general(q, kn, NT_DIMS, preferred_element_type=F32) * scale
            s_c = jnp.where(ok_c, s_c - slope * d_c.astype(F32), -jnp.inf)
            s_n = jnp.where(ok_n, s_n - slope * d_n.astype(F32), -jnp.inf)
            m = jnp.maximum(jnp.max(s_c, axis=-1, keepdims=True), jnp.max(s_n, axis=-1, keepdims=True))
            p_c = jnp.exp(s_c - m)
            p_n = jnp.exp(s_n - m)
            den = jnp.sum(p_c, axis=-1, keepdims=True) + jnp.sum(p_n, axis=-1, keepdims=True)
            o = (jnp.dot((p_c / den).astype(BF16), vc, preferred_element_type=F32)
                 + jnp.dot((p_n / den).astype(BF16), vn, preferred_element_type=F32))
            outs.append(o)
            lses.append(m + jnp.log(den))
        o_ref[:, h * HEAD_DIM:(h + 1) * HEAD_DIM] = _combine_groups(outs, lses)


def _attn_sample(qkv, slopes, caches, dec_batch, dec_seq):
    att_w = N_HEADS_ATT * HEAD_DIM
    hw = HEADS_PER_GROUP * HEAD_DIM
    ins, specs = [], []
    for g in range(N_GROUPS):
        for c in caches[g]:
            win = c.shape[2]
            if g == 2:
                dil = DILATIONS[g]
                assert dec_seq <= dil and dec_seq % 8 == 0
                ins.append(c.reshape(1, dec_batch, win // dil, dil, HEADS_PER_GROUP, HEAD_DIM))
                specs.append(pl.BlockSpec((None, None, win // dil, dec_seq, HEADS_PER_GROUP, HEAD_DIM),
                                          lambda b: (0, b, 0, 0, 0, 0)))
            else:
                ins.append(c)
                specs.append(pl.BlockSpec((None, None, win, HEADS_PER_GROUP, HEAD_DIM), lambda b: (0, b, 0, 0, 0)))
    return pl.pallas_call(
        functools.partial(_attn_sample_kernel, dec_seq=dec_seq),
        grid=(dec_batch,),
        in_specs=[pl.BlockSpec(memory_space=pltpu.SMEM),
                  pl.BlockSpec((dec_seq, 3 * att_w), lambda b: (b, 0))] + specs,
        out_specs=pl.BlockSpec((dec_seq, hw), lambda b: (b, 0)),
        out_shape=jax.ShapeDtypeStruct((dec_batch * dec_seq, hw), F32),
        compiler_params=_params("parallel"),
        name="attn_sample",
    )(slopes, qkv, *ins)


def _sgu_kernel(*refs, block_rows, parts, want_vn):
    su_refs, sv_refs = refs[:parts], refs[parts:2 * parts]
    lg_ref, lb_ref, w_ref, b_ref, sg_ref = refs[2 * parts:2 * parts + 5]

    def whole(rs):
        return rs[0][...] if parts == 1 else jnp.concatenate([r[...] for r in rs], axis=1)

    u = _gelu(whole(su_refs))
    v = _gelu(whole(sv_refs))
    mu = jnp.mean(v, axis=-1, keepdims=True)
    vc = v - mu
    vn = vc * lax.rsqrt(jnp.mean(vc * vc, axis=-1, keepdims=True) + EPS) * lg_ref[...] + lb_ref[...]
    if want_vn:
        refs[2 * parts + 5][...] = vn
    r = lax.broadcasted_iota(jnp.int32, (CHUNK, CHUNK), 0)
    c = lax.broadcasted_iota(jnp.int32, (CHUNK, CHUNK), 1)
    keep = (c <= r) & ((r // block_rows) == (c // block_rows))
    ch = vn.shape[1] // SG_GROUPS
    for g in range(SG_GROUPS):
        w = jnp.where(keep, w_ref[g], 0.0).astype(BF16)
        mixed = jnp.dot(w, vn[:, g * ch:(g + 1) * ch].astype(BF16), preferred_element_type=F32) + b_ref[:, g:g + 1]
        sg_ref[:, g * ch:(g + 1) * ch] = (u[:, g * ch:(g + 1) * ch] * mixed).astype(sg_ref.dtype)


def _sgu(proj, su_col, sv_col, d_sg, ln_g, ln_b, w, bias_t, block_rows, want_vn):
    m = proj.shape[0]
    out_shape = [jax.ShapeDtypeStruct((m, d_sg), BF16)]
    out_specs = [pl.BlockSpec((CHUNK, d_sg), lambda i: (i, 0))]
    if want_vn:
        out_shape.append(jax.ShapeDtypeStruct((m, d_sg), F32))
        out_specs.append(pl.BlockSpec((CHUNK, d_sg), lambda i: (i, 0)))
    unit = math.gcd(math.gcd(su_col, sv_col), d_sg)
    parts = d_sg // unit
    col_specs = [pl.BlockSpec((CHUNK, unit), lambda i, cb=c0 // unit + k: (i, cb))
                 for c0 in (su_col, sv_col) for k in range(parts)]
    return pl.pallas_call(
        functools.partial(_sgu_kernel, block_rows=block_rows, parts=parts, want_vn=want_vn),
        grid=(m // CHUNK,),
        in_specs=col_specs + [pl.BlockSpec((1, d_sg), lambda i: (0, 0)),
                              pl.BlockSpec((1, d_sg), lambda i: (0, 0)),
                              pl.BlockSpec((SG_GROUPS, CHUNK, CHUNK), lambda i: (0, 0, 0)),
                              pl.BlockSpec((CHUNK, SG_GROUPS), lambda i: (0, 0))],
        out_specs=out_specs,
        out_shape=out_shape,
        compiler_params=_params("parallel"),
        name="sgu",
    )(*([proj] * (2 * parts)), ln_g.reshape(1, d_sg), ln_b.reshape(1, d_sg), w, bias_t)


def _take_top(s, key_id, n_take):
    rows, cols = s.shape
    slot = lax.broadcasted_iota(jnp.int32, (n_take, cols), 0)
    rank = jnp.full((rows, cols), float(n_take), F32)
    top = jnp.zeros((n_take, cols), F32)
    for r in range(n_take):
        m = jnp.max(s, axis=0, keepdims=True)
        first = jnp.min(jnp.where(s == m, key_id, float(rows)), axis=0, keepdims=True)
        sel = key_id == first
        rank = jnp.where(sel, float(r), rank)
        s = jnp.where(sel, -jnp.inf, s)
        top = jnp.where(slot == r, m, top)
    return top, rank


def _peer_route_kernel(q_ref, sk_ref, rk_ref, e1_ref, lk_ref, e0_ref):
    tt = q_ref.shape[0]
    half = q_ref.shape[1] // 2
    nk = sk_ref.shape[1]
    key_id = lax.broadcasted_iota(jnp.int32, (nk, tt), 0).astype(F32)
    scores, tops, ranks = [], [], []
    for p in range(2):
        s = lax.dot_general(sk_ref[p].astype(BF16), q_ref[:, p * half:(p + 1) * half].astype(BF16),
                            NT_DIMS, preferred_element_type=F32)
        top, rank = _take_top(s, key_id, PEER_TOPK)
        scores.append(s)
        tops.append(top)
        ranks.append(rank)
    top0, top1 = tops

    sub = lax.broadcasted_iota(jnp.int32, (8, tt), 0)
    subf = sub.astype(F32)
    pieces, flat = [], []
    for a in range(8):
        n_b = PEER_TOPK // (a + 1)
        for b0 in range(0, n_b, 8):
            c = top0[a:a + 1, :] + top1[b0:b0 + 8, :]
            pieces.append(jnp.where(sub < n_b - b0, c, -jnp.inf))
            flat.append(subf + float(a * PEER_TOPK + b0))
    pieces.append(top0[8:16, :] + top1[0:1, :])
    flat.append((subf + 8.0) * float(PEER_TOPK))
    cand = list(pieces)
    taken = [jnp.zeros((8, tt), F32) for _ in pieces]
    big = float(PEER_TOPK * PEER_TOPK)
    for _ in range(PEER_TOPK):
        m = jnp.max(functools.reduce(jnp.maximum, cand), axis=0, keepdims=True)
        first = jnp.min(functools.reduce(jnp.minimum, [jnp.where(c == m, f, big) for c, f in zip(cand, flat)]),
                        axis=0, keepdims=True)
        for i, f in enumerate(flat):
            sel = f == first
            cand[i] = jnp.where(sel, -jnp.inf, cand[i])
            taken[i] = jnp.where(sel, 1.0, taken[i])
    best = top0[0:1, :] + top1[0:1, :]
    z = functools.reduce(jnp.add, [jnp.sum(jnp.where(t > 0.0, jnp.exp(c - best), 0.0), axis=0, keepdims=True)
                                   for t, c in zip(taken, pieces)])
    counts = [jnp.sum(taken[0] + taken[1], axis=0, keepdims=True)]
    counts += [jnp.sum(taken[a + 1], axis=0, keepdims=True) for a in range(1, 8)]
    counts += [taken[-1][a:a + 1, :] for a in range(8)]
    lk = jnp.zeros((nk, tt), F32)
    for a in range(PEER_TOPK):
        lk = jnp.where(ranks[0] == float(a), counts[a], lk)
    rk_ref[...] = ranks[1]
    e1_ref[...] = jnp.exp(scores[1] - top1[0:1, :])
    lk_ref[...] = lk
    e0_ref[...] = jnp.exp(scores[0] - top0[0:1, :]) / z


def _peer_route(qp, sub_keys, tt):
    m = qp.shape[0]
    nh, _, nk, half = sub_keys.shape
    shp = jax.ShapeDtypeStruct((nh, nk, m), F32)
    ospec = pl.BlockSpec((None, nk, tt), lambda i, h: (h, 0, i))
    return pl.pallas_call(
        _peer_route_kernel,
        grid=(m // tt, nh),
        in_specs=[pl.BlockSpec((tt, 2 * half), lambda i, h: (i, h)),
                  pl.BlockSpec((None, 2, nk, half), lambda i, h: (h, 0, 0, 0))],
        out_specs=[ospec] * 4,
        out_shape=[shp] * 4,
        compiler_params=_params("parallel", "arbitrary"),
        name="peer_route",
    )(qp, sub_keys)


MXU_COLS = 256


def _peer_dense_kernel(h_ref, u_ref, v_ref, rk_ref, e1_ref, lk_ref, e0_ref, o_ref, act_ref, g_ref, *,
                       n_chunks, n_items):
    s = pl.program_id(0)
    _, tt, ec = act_ref.shape
    nk = rk_ref.shape[1]
    d = v_ref.shape[1]
    first_key = (jnp.clip(s - 1, 0, n_items - 1) % n_chunks) * (ec // nk)
    down_chunk = jnp.clip(s - 2, 0, n_items - 1) % n_chunks

    @pl.when(s == 0)
    def _():
        act_ref[...] = jnp.zeros(act_ref.shape, act_ref.dtype)
        g_ref[...] = jnp.zeros(g_ref.shape, g_ref.dtype)

    @pl.when(down_chunk == 0)
    def _():
        o_ref[...] = jnp.zeros(o_ref.shape, o_ref.dtype)

    n_tiles = d // MXU_COLS
    pieces = [(ii, tb) for ii in range(ec // nk) for tb in range(tt // nk)]
    per_tile = -(-len(pieces) // n_tiles)
    half = ec // 2

    def step(rs, ws):
        key_rows = {}

        def key_row(ref, hd, ii):
            if (id(ref), hd, ii) not in key_rows:
                key_rows[id(ref), hd, ii] = ref[hd, pl.ds(first_key + ii, 1), :]
            return key_rows[id(ref), hd, ii]

        for n in range(n_tiles):
            if n == 0:
                act_ref[ws, :, :half] = lax.dot_general(h_ref[...], u_ref[:half, :], NT_DIMS,
                                                        preferred_element_type=F32)
            if n == n_tiles // 2:
                act_ref[ws, :, half:] = lax.dot_general(h_ref[...], u_ref[half:, :], NT_DIMS,
                                                        preferred_element_type=F32)
            cols = slice(n * MXU_COLS, (n + 1) * MXU_COLS)
            o_ref[:, cols] += jnp.dot(g_ref[rs], v_ref[:, cols], preferred_element_type=F32)
            for ii, tb in pieces[n * per_tile:(n + 1) * per_tile]:
                toks = slice(tb * nk, (tb + 1) * nk)
                exps = slice(ii * nk, (ii + 1) * nk)
                w = None
                for hd in range(rk_ref.shape[0]):
                    take = rk_ref[hd, :, toks] < key_row(lk_ref, hd, ii)[:, toks]
                    term = jnp.where(take, e1_ref[hd, :, toks], 0.0) * key_row(e0_ref, hd, ii)[:, toks]
                    w = term if w is None else w + term
                g_ref[ws, toks, exps] = (w.T * _gelu(act_ref[rs, toks, exps])).astype(BF16)

    @pl.when(s % 2 == 0)
    def _():
        step(1, 0)

    @pl.when(s % 2 == 1)
    def _():
        step(0, 1)


def _peer_dense(h2, u_bf, v_bf, route, tt, ec):
    m, d = h2.shape
    n_chunks = u_bf.shape[0] // ec
    n_items = (m // tt) * n_chunks
    nh, nk, _ = route[0].shape

    def up_item(s):
        return jnp.minimum(s, n_items - 1)

    def gate_item(s):
        return jnp.clip(s - 1, 0, n_items - 1)

    def down_item(s):
        return jnp.clip(s - 2, 0, n_items - 1)

    rspec = pl.BlockSpec((nh, nk, tt), lambda s: (0, 0, gate_item(s) // n_chunks), pipeline_mode=pl.Buffered(1))
    return pl.pallas_call(
        functools.partial(_peer_dense_kernel, n_chunks=n_chunks, n_items=n_items),
        grid=(n_items + 2,),
        in_specs=[pl.BlockSpec((tt, d), lambda s: (up_item(s) // n_chunks, 0), pipeline_mode=pl.Buffered(1)),
                  pl.BlockSpec((ec, d), lambda s: (up_item(s) % n_chunks, 0)),
                  pl.BlockSpec((ec, d), lambda s: (down_item(s) % n_chunks, 0)),
                  rspec, rspec, rspec, rspec],
        out_specs=pl.BlockSpec((tt, d), lambda s: (down_item(s) // n_chunks, 0)),
        out_shape=jax.ShapeDtypeStruct((m, d), F32),
        scratch_shapes=[pltpu.VMEM((2, tt, ec), F32), pltpu.VMEM((2, tt, ec), BF16)],
        compiler_params=_params("arbitrary"),
        name="peer_dense",
    )(h2, u_bf, v_bf, *route)


def _final_kernel(x_ref, p_ref, gt_ref, g_ref, o_ref):
    x = x_ref[...] + gt_ref[...] * p_ref[...]
    o_ref[...] = x * lax.rsqrt(jnp.mean(x * x, axis=-1, keepdims=True) + EPS) * g_ref[...]


def _final(x, peer, mod, gt_idx, g, rows, tm):
    m, d = x.shape
    return pl.pallas_call(
        _final_kernel,
        grid=(m // tm, 1),
        in_specs=[pl.BlockSpec((tm, d), lambda i, j: (i, 0)),
                  pl.BlockSpec((tm, d), lambda i, j: (i, 0)),
                  rows.spec(tm, d, gt_idx),
                  pl.BlockSpec((1, d), lambda i, j: (0, 0))],
        out_specs=pl.BlockSpec((tm, d), lambda i, j: (i, 0)),
        out_shape=jax.ShapeDtypeStruct((m, d), F32),
        compiler_params=_params("parallel", "arbitrary"),
        name="final",
    )(x, peer, mod, g.reshape(1, d))


def _tile(n, want):
    t = min(n, want)
    assert n % t == 0, (n, want)
    return t


def _layer(x, mod, rows, attend, sgu_block_rows, want_vn, p):
    m, d = x.shape
    d_sg = d // 2
    att_w = N_HEADS_ATT * HEAD_DIM
    tm = _tile(m, 1024)
    tm_row = _tile(m, 256)
    h1 = _normmod(x, p["norm1_g"], mod, 1, 0, rows, tm_row)
    proj = _matmul(h1, p["w_in"], tm, 512, "inproj")
    att = attend(proj)
    sg_out = _sgu(proj, 3 * att_w, 3 * att_w + d_sg, d_sg, p["sg_ln_g"], p["sg_ln_b"],
                  p["sg_w_eff"][sgu_block_rows], p["sg_b_eff"][sgu_block_rows], sgu_block_rows, want_vn)
    sg = sg_out[0]
    merged = _merge(att.astype(BF16), sg, proj, 3 * att_w + 2 * d_sg, 3 * att_w + 2 * d_sg + d,
                    p["w_pa"], p["w_pb"], tm, 512)
    x1 = _outproj(merged, p["w_o"], x, mod, 2, rows, tm, 512)
    h2 = _normmod(x1, p["norm2_g"], mod, 4, 3, rows, tm_row)
    qp = _matmul(h2, p["peer_wq"], tm, 512, "peer_q")
    tt = _tile(m, 512)
    route = _peer_route(qp, p["peer_subkeys"], tt)
    peer = _peer_dense(h2, p["peer_u_bf"], p["peer_v_bf"], route, tt, _tile(p["peer_u_bf"].shape[0], 512))
    y = _final(x1, peer, mod, 5, p["final_g"], rows, tm_row)
    return y, proj, (sg_out[1] if want_vn else None)


def kernel(x_prompt, x_sample, cache_k0, cache_v0, cache_k1, cache_v1, cache_k2, cache_v2, c_prompt, c_sample,
           ada_w, ada_b, norm1_g, norm2_g, w_in, sg_ln_g, sg_ln_b, sg_w, sg_b, w_pa, w_pb, w_o, peer_wq,
           peer_subkeys, peer_u, peer_v, final_g):
    batch, seq, d = x_prompt.shape
    dec_batch, dec_seq, _ = x_sample.shape
    depth = ada_w.shape[0]
    assert depth == 1, "single-layer step"
    att_w = N_HEADS_ATT * HEAD_DIM
    hw = HEADS_PER_GROUP * HEAD_DIM

    slopes = jnp.exp2(-8.0 * jnp.arange(1, N_HEADS_ATT + 1, dtype=F32) / N_HEADS_ATT).reshape(N_GROUPS, HEADS_PER_GROUP)
    sg_wl, sg_bl = sg_w[0], sg_b[0]
    reps = CHUNK // dec_seq
    p = dict(norm1_g=norm1_g[0], norm2_g=norm2_g[0], w_in=w_in[0], sg_ln_g=sg_ln_g[0], sg_ln_b=sg_ln_b[0],
             w_pa=w_pa[0], w_pb=w_pb[0], w_o=w_o[0], peer_wq=peer_wq[0], peer_subkeys=peer_subkeys[0],
             peer_u_bf=peer_u[0].astype(BF16), peer_v_bf=peer_v[0].astype(BF16), final_g=final_g,
             sg_w_eff={CHUNK: sg_wl, dec_seq: jnp.tile(sg_wl[:, :dec_seq, :dec_seq], (1, reps, reps))},
             sg_b_eff={CHUNK: sg_bl.T, dec_seq: jnp.tile(sg_bl.T[:dec_seq], (reps, 1))})

    c_all = jnp.concatenate([c_prompt, c_sample], axis=0)
    mod = _modulation(c_all, ada_w[0], ada_b[0])
    mod_p = mod[:batch].reshape(batch, 1, N_MOD * d)
    mod_s = jnp.repeat(mod[batch:], dec_seq, axis=0)

    y_p, proj_p, _ = _layer(
        x_prompt.reshape(batch * seq, d), mod_p, _Rows(True, seq),
        lambda proj: _attn_prompt(proj, slopes, batch, seq), CHUNK, False, p)

    caches = [(cache_k0, cache_v0), (cache_k1, cache_v1), (cache_k2, cache_v2)]
    y_s, proj_s, vn_s = _layer(
        x_sample.reshape(dec_batch * dec_seq, d), mod_s, _Rows(False, dec_seq),
        lambda proj: _attn_sample(proj, slopes, caches, dec_batch, dec_seq), dec_seq, True, p)

    outs = [y_p.reshape(batch, seq, d), y_s.reshape(dec_batch, dec_seq, d)]
    kp = proj_p.reshape(batch, seq, -1)
    ks = proj_s.reshape(dec_batch, dec_seq, -1)
    for g in range(N_GROUPS):
        keep = min(WINDOWS[g], seq)
        for src, rows_from in ((kp, seq - keep), (ks, 0)):
            for part in (1, 2):
                c0 = part * att_w + g * hw
                blk = src[:, rows_from:, c0:c0 + hw]
                outs.append(blk.reshape(1, blk.shape[0], blk.shape[1], HEADS_PER_GROUP, HEAD_DIM))
    outs.append(vn_s.reshape(1, dec_batch, dec_seq, d // 2))
    return tuple(outs)
```

```python
import functools
import math

import jax
import jax.numpy as jnp
from jax import lax
from jax.experimental import pallas as pl
from jax.experimental.pallas import tpu as pltpu

F32 = jnp.float32
BF16 = jnp.bfloat16
EPS = 1e-6

VMEM_LIMIT_BYTES = 58 * 1024 * 1024
LANES = 128

HEAD_DIM = 128
N_GROUPS = 3
HEADS_PER_GROUP = 8
N_HEADS_ATT = N_GROUPS * HEADS_PER_GROUP
WINDOWS = (128, 512, 2048)
DILATIONS = (1, 4, 16)
ATT_SPAN = 128
CHUNK = 128
SG_GROUPS = 8
PEER_HEADS = 8
PEER_NKEYS = 128
PEER_TOPK = 16
N_MOD = 6

NT_DIMS = (((1,), (1,)), ((), ()))


def _params(*sem):
    return pltpu.CompilerParams(dimension_semantics=sem, vmem_limit_bytes=VMEM_LIMIT_BYTES)


def _gelu(x):
    return x * (lax.erf(x * (1.0 / math.sqrt(2.0))) + 1.0) * 0.5


class _Rows:
    def __init__(self, shared, rows_per_batch):
        self.shared = shared
        self.rows_per_batch = rows_per_batch

    def spec(self, tm, tn, col_block0=0):
        if self.shared:
            tpb = self.rows_per_batch // tm
            return pl.BlockSpec((None, 1, tn), lambda i, j: (i // tpb, 0, j + col_block0))
        return pl.BlockSpec((tm, tn), lambda i, j: (i, j + col_block0))


def _mod_kernel(c_ref, w_ref, b_ref, o_ref):
    c = c_ref[...]
    a = (c * jax.nn.sigmoid(c)).astype(BF16)
    o_ref[...] = jnp.dot(a, w_ref[...].astype(BF16), preferred_element_type=F32) + b_ref[...]


def _modulation(c, w, b, tn=1024):
    n, d = c.shape
    cols = w.shape[1]
    return pl.pallas_call(
        _mod_kernel,
        grid=(cols // tn,),
        in_specs=[pl.BlockSpec((n, d), lambda j: (0, 0)),
                  pl.BlockSpec((d, tn), lambda j: (0, j)),
                  pl.BlockSpec((1, tn), lambda j: (0, j))],
        out_specs=pl.BlockSpec((n, tn), lambda j: (0, j)),
        out_shape=jax.ShapeDtypeStruct((n, cols), F32),
        compiler_params=_params("parallel"),
        name="modulation",
    )(c, w, b.reshape(1, cols))


def _normmod_kernel(x_ref, g_ref, sc_ref, sh_ref, o_ref):
    x = x_ref[...]
    y = x * lax.rsqrt(jnp.mean(x * x, axis=-1, keepdims=True) + EPS)
    o_ref[...] = ((y * g_ref[...]) * (1.0 + sc_ref[...]) + sh_ref[...]).astype(o_ref.dtype)


def _normmod(x, g, mod, sc_idx, sh_idx, rows, tm):
    m, d = x.shape
    return pl.pallas_call(
        _normmod_kernel,
        grid=(m // tm, 1),
        in_specs=[pl.BlockSpec((tm, d), lambda i, j: (i, 0)),
                  pl.BlockSpec((1, d), lambda i, j: (0, 0)),
                  rows.spec(tm, d, sc_idx),
                  rows.spec(tm, d, sh_idx)],
        out_specs=pl.BlockSpec((tm, d), lambda i, j: (i, 0)),
        out_shape=jax.ShapeDtypeStruct((m, d), BF16),
        compiler_params=_params("parallel", "arbitrary"),
        name="normmod",
    )(x, g.reshape(1, d), mod, mod)


def _mm_kernel(a_ref, w_ref, o_ref):
    o_ref[...] = jnp.dot(a_ref[...], w_ref[...].astype(BF16), preferred_element_type=F32).astype(o_ref.dtype)


def _matmul(a, w, tm, tn, name):
    m, k = a.shape
    n = w.shape[1]
    return pl.pallas_call(
        _mm_kernel,
        grid=(m // tm, n // tn),
        in_specs=[pl.BlockSpec((tm, k), lambda i, j: (i, 0)),
                  pl.BlockSpec((k, tn), lambda i, j: (0, j))],
        out_specs=pl.BlockSpec((tm, tn), lambda i, j: (i, j)),
        out_shape=jax.ShapeDtypeStruct((m, n), F32),
        compiler_params=_params("parallel", "arbitrary"),
        name=name,
    )(a, w)


def _mm_cast_kernel(a_ref, w_ref, t0_ref, t1_ref, o_ref, c0_ref, c1_ref):
    o_ref[...] = jnp.dot(a_ref[...], w_ref[...].astype(BF16), preferred_element_type=F32).astype(o_ref.dtype)
    c0_ref[...] = t0_ref[...].astype(BF16)
    c1_ref[...] = t1_ref[...].astype(BF16)


def _matmul_and_cast(a, w, tables, tm, tn, name):
    m, k = a.shape
    n = w.shape[1]
    rows, width = tables[0].shape
    steps_j = n // tn
    n_steps = (m // tm) * steps_j
    rb = 16
    while rows // rb > n_steps:
        rb *= 2
    assert rows % rb == 0, (rows, rb)
    n_blocks = rows // rb

    def tmap(i, j):
        return (jnp.minimum(i * steps_j + j, n_blocks - 1), 0)

    return pl.pallas_call(
        _mm_cast_kernel,
        grid=(m // tm, steps_j),
        in_specs=[pl.BlockSpec((tm, k), lambda i, j: (i, 0)),
                  pl.BlockSpec((k, tn), lambda i, j: (0, j)),
                  pl.BlockSpec((rb, width), tmap),
                  pl.BlockSpec((rb, width), tmap)],
        out_specs=[pl.BlockSpec((tm, tn), lambda i, j: (i, j)),
                   pl.BlockSpec((rb, width), tmap),
                   pl.BlockSpec((rb, width), tmap)],
        out_shape=[jax.ShapeDtypeStruct((m, n), F32),
                   jax.ShapeDtypeStruct((rows, width), BF16),
                   jax.ShapeDtypeStruct((rows, width), BF16)],
        compiler_params=_params("arbitrary", "arbitrary"),
        name=name,
    )(a, w, *tables)


def _merge_kernel(att_ref, sg_ref, ga_ref, gb_ref, wa_ref, wb_ref, o_ref):
    pa = jnp.dot(att_ref[...], wa_ref[...].astype(BF16), preferred_element_type=F32)
    pb = jnp.dot(sg_ref[...], wb_ref[...].astype(BF16), preferred_element_type=F32)
    o_ref[...] = (jax.nn.sigmoid(ga_ref[...]) * pa + jax.nn.sigmoid(gb_ref[...]) * pb).astype(o_ref.dtype)


def _merge(att, sg, proj, ga_col, gb_col, w_pa, w_pb, tm, tn):
    m, ka = att.shape
    kb = sg.shape[1]
    n = w_pa.shape[1]
    ga0, gb0 = ga_col // tn, gb_col // tn
    return pl.pallas_call(
        _merge_kernel,
        grid=(m // tm, n // tn),
        in_specs=[pl.BlockSpec((tm, ka), lambda i, j: (i, 0)),
                  pl.BlockSpec((tm, kb), lambda i, j: (i, 0)),
                  pl.BlockSpec((tm, tn), lambda i, j: (i, j + ga0)),
                  pl.BlockSpec((tm, tn), lambda i, j: (i, j + gb0)),
                  pl.BlockSpec((ka, tn), lambda i, j: (0, j)),
                  pl.BlockSpec((kb, tn), lambda i, j: (0, j))],
        out_specs=pl.BlockSpec((tm, tn), lambda i, j: (i, j)),
        out_shape=jax.ShapeDtypeStruct((m, n), BF16),
        compiler_params=_params("parallel", "arbitrary"),
        name="merge",
    )(att, sg, proj, proj, w_pa, w_pb)


def _outproj_kernel(a_ref, w_ref, x_ref, gt_ref, o_ref):
    acc = jnp.dot(a_ref[...], w_ref[...].astype(BF16), preferred_element_type=F32)
    o_ref[...] = x_ref[...] + gt_ref[...] * acc


def _outproj(a, w, x, mod, gt_idx, rows, tm, tn):
    m, k = a.shape
    n = w.shape[1]
    return pl.pallas_call(
        _outproj_kernel,
        grid=(m // tm, n // tn),
        in_specs=[pl.BlockSpec((tm, k), lambda i, j: (i, 0)),
                  pl.BlockSpec((k, tn), lambda i, j: (0, j)),
                  pl.BlockSpec((tm, tn), lambda i, j: (i, j)),
                  rows.spec(tm, tn, gt_idx * (n // tn))],
        out_specs=pl.BlockSpec((tm, tn), lambda i, j: (i, j)),
        out_shape=jax.ShapeDtypeStruct((m, n), F32),
        compiler_params=_params("parallel", "arbitrary"),
        name="outproj",
    )(a, w, x, mod)


def _softmax_parts(s):
    m = jnp.max(s, axis=-1, keepdims=True)
    p = jnp.exp(s - m)
    return m, p, jnp.sum(p, axis=-1, keepdims=True)


def _combine_groups(outs, lses):
    top = functools.reduce(jnp.maximum, lses)
    es = [jnp.exp(l - top) for l in lses]
    tot = functools.reduce(jnp.add, es)
    return functools.reduce(jnp.add, [(e / tot) * o for e, o in zip(es, outs)])


def _attn_prompt_kernel(sl_ref, *refs, seq):
    q_refs, k_refs, v_refs = refs[0:9:3], refs[1:9:3], refs[2:9:3]
    o_ref, og_ref, lg_ref = refs[9], refs[10], refs[11]
    h = pl.program_id(1)
    scale = HEAD_DIM ** -0.5
    span = ATT_SPAN
    back1 = lax.broadcasted_iota(jnp.int32, (span, span), 0) - lax.broadcasted_iota(jnp.int32, (span, span), 1)
    back2 = (lax.broadcasted_iota(jnp.int32, (span, 2 * span), 0) + span
             - lax.broadcasted_iota(jnp.int32, (span, 2 * span), 1))
    for g in range(N_GROUPS):
        dil = DILATIONS[g]
        n_blocks = seq // dil // span
        step_bias = sl_ref[g, h] * float(dil)
        bias1 = jnp.where(back1 >= 0, -(step_bias * back1.astype(F32)), -jnp.inf)
        bias2 = jnp.where((back2 >= 0) & (back2 <= span), -(step_bias * back2.astype(F32)), -jnp.inf)
        for res in range(dil):
            k_prev = v_prev = None
            for blk in range(n_blocks):
                start = res + dil * span * blk
                rows = pl.ds(start, span) if dil == 1 else pl.ds(start, span, stride=dil)
                q = q_refs[g][rows, :].astype(BF16)
                k_cur = k_refs[g][rows, :].astype(BF16)
                v_cur = v_refs[g][rows, :].astype(BF16)
                if blk == 0:
                    k, v, bias = k_cur, v_cur, bias1
                else:
                    k = jnp.concatenate([k_prev, k_cur], axis=0)
                    v = jnp.concatenate([v_prev, v_cur], axis=0)
                    bias = bias2
                s = lax.dot_general(q, k, NT_DIMS, preferred_element_type=F32) * scale + bias
                m, p, den = _softmax_parts(s)
                og_ref[g, rows, :] = jnp.dot((p / den).astype(BF16), v, preferred_element_type=F32)
                lg_ref[g, rows, :] = jnp.broadcast_to(m + jnp.log(den), (span, HEAD_DIM))
                k_prev, v_prev = k_cur, v_cur

    def combine(i, carry):
        rows = pl.ds(pl.multiple_of(i * span, span), span)
        o_ref[rows, :] = _combine_groups([og_ref[g, rows, :] for g in range(N_GROUPS)],
                                         [lg_ref[g, rows, :] for g in range(N_GROUPS)]).astype(o_ref.dtype)
        return carry

    lax.fori_loop(0, seq // span, combine, 0)


def _attn_prompt(qkv, slopes, batch, seq):
    ncols = qkv.shape[1]
    assert seq % (max(DILATIONS) * ATT_SPAN) == 0, "every residue class needs whole query blocks"
    qkv3 = qkv.reshape(batch, seq, ncols)
    in_specs = [pl.BlockSpec(memory_space=pltpu.SMEM)]
    for g in range(N_GROUPS):
        for part in range(3):
            cb = part * N_HEADS_ATT + g * HEADS_PER_GROUP
            in_specs.append(pl.BlockSpec((None, seq, HEAD_DIM), lambda b, h, cb=cb: (b, 0, cb + h)))
    out = pl.pallas_call(
        functools.partial(_attn_prompt_kernel, seq=seq),
        grid=(batch, HEADS_PER_GROUP),
        in_specs=in_specs,
        out_specs=pl.BlockSpec((None, seq, HEAD_DIM), lambda b, h: (b, 0, h)),
        out_shape=jax.ShapeDtypeStruct((batch, seq, HEADS_PER_GROUP * HEAD_DIM), BF16),
        scratch_shapes=[pltpu.VMEM((N_GROUPS, seq, HEAD_DIM), F32),
                        pltpu.VMEM((N_GROUPS, seq, HEAD_DIM), F32)],
        compiler_params=_params("parallel", "arbitrary"),
        name="attn_prompt",
    )(slopes, *([qkv3] * 9))
    return out.reshape(batch * seq, HEADS_PER_GROUP * HEAD_DIM)


def _attn_sample_kernel(sl_ref, qkv_ref, k0, v0, k1, v1, k2, v2, o_ref, *, dec_seq):
    scale = HEAD_DIM ** -0.5
    att_w = N_HEADS_ATT * HEAD_DIM
    caches = ((k0, v0), (k1, v1), (k2, v2))
    t_new = lax.broadcasted_iota(jnp.int32, (dec_seq, dec_seq), 0)
    c_new = lax.broadcasted_iota(jnp.int32, (dec_seq, dec_seq), 1)
    for h in range(HEADS_PER_GROUP):
        outs, lses = [], []
        for g in range(N_GROUPS):
            dil, win = DILATIONS[g], WINDOWS[g]
            col = (g * HEADS_PER_GROUP + h) * HEAD_DIM
            q = qkv_ref[:, col:col + HEAD_DIM].astype(BF16)
            kn = qkv_ref[:, att_w + col:att_w + col + HEAD_DIM].astype(BF16)
            vn = qkv_ref[:, 2 * att_w + col:2 * att_w + col + HEAD_DIM].astype(BF16)
            kc_ref, vc_ref = caches[g]
            head_rows = pl.ds(h, kc_ref.shape[-2] // HEADS_PER_GROUP, stride=HEADS_PER_GROUP)
            if g == 2:
                kc = kc_ref[:, head_rows, :].reshape(-1, HEAD_DIM).astype(BF16)
                vc = vc_ref[:, head_rows, :].reshape(-1, HEAD_DIM).astype(BF16)
            else:
                kc = kc_ref[head_rows, :].astype(BF16)
                vc = vc_ref[head_rows, :].astype(BF16)
            rows = kc.shape[0]
            t_c = lax.broadcasted_iota(jnp.int32, (dec_seq, rows), 0)
            c_c = lax.broadcasted_iota(jnp.int32, (dec_seq, rows), 1)
            if g == 2:
                e_c = ((c_c >> 3) << 4) + (c_c & 7)
            else:
                e_c = c_c
            d_c = win + t_c - e_c
            ok_c = ((d_c & (dil - 1)) == 0) & (d_c <= ATT_SPAN * dil)
            d_n = t_new - c_new
            ok_n = (d_n >= 0) & ((d_n & (dil - 1)) == 0)
            slope = sl_ref[g, h]
            s_c = lax.dot_general(q, kc, NT_DIMS, preferred_element_type=F32) * scale
            s_n = lax.dot_general(q, kn, NT_DIMS, preferred_element_type=F32) * scale
            s_c = jnp.where(ok_c, s_c - slope * d_c.astype(F32), -jnp.inf)
            s_n = jnp.where(ok_n, s_n - slope * d_n.astype(F32), -jnp.inf)
            m = jnp.maximum(jnp.max(s_c, axis=-1, keepdims=True), jnp.max(s_n, axis=-1, keepdims=True))
            p_c = jnp.exp(s_c - m)
            p_n = jnp.exp(s_n - m)
            den = jnp.sum(p_c, axis=-1, keepdims=True) + jnp.sum(p_n, axis=-1, keepdims=True)
            o = (jnp.dot((p_c / den).astype(BF16), vc, preferred_element_type=F32)
                 + jnp.dot((p_n / den).astype(BF16), vn, preferred_element_type=F32))
            outs.append(o)
            lses.append(m + jnp.log(den))
        o_ref[:, h * HEAD_DIM:(h + 1) * HEAD_DIM] = _combine_groups(outs, lses)


def _attn_sample(qkv, slopes, caches, dec_batch, dec_seq):
    att_w = N_HEADS_ATT * HEAD_DIM
    hw = HEADS_PER_GROUP * HEAD_DIM
    ins, specs = [], []
    for g in range(N_GROUPS):
        for c in caches[g]:
            win = c.shape[2]
            if g == 2:
                dil = DILATIONS[g]
                assert dec_seq <= dil and dec_seq % 8 == 0
                ins.append(c.reshape(1, dec_batch, win // dil, dil * HEADS_PER_GROUP, HEAD_DIM))
                specs.append(pl.BlockSpec((None, None, win // dil, dec_seq * HEADS_PER_GROUP, HEAD_DIM),
                                          lambda b: (0, b, 0, 0, 0)))
            else:
                ins.append(c.reshape(1, dec_batch, win * HEADS_PER_GROUP, HEAD_DIM))
                specs.append(pl.BlockSpec((None, None, win * HEADS_PER_GROUP, HEAD_DIM), lambda b: (0, b, 0, 0)))
    return pl.pallas_call(
        functools.partial(_attn_sample_kernel, dec_seq=dec_seq),
        grid=(dec_batch,),
        in_specs=[pl.BlockSpec(memory_space=pltpu.SMEM),
                  pl.BlockSpec((dec_seq, 3 * att_w), lambda b: (b, 0))] + specs,
        out_specs=pl.BlockSpec((dec_seq, hw), lambda b: (b, 0)),
        out_shape=jax.ShapeDtypeStruct((dec_batch * dec_seq, hw), F32),
        compiler_params=_params("parallel"),
        name="attn_sample",
    )(slopes, qkv, *ins)


def _sgu_kernel(*refs, block_rows, parts, want_vn):
    su_refs, sv_refs = refs[:parts], refs[parts:2 * parts]
    lg_ref, lb_ref, w_ref, b_ref, sg_ref = refs[2 * parts:2 * parts + 5]

    def whole(rs):
        return rs[0][...] if parts == 1 else jnp.concatenate([r[...] for r in rs], axis=1)

    u = _gelu(whole(su_refs))
    v = _gelu(whole(sv_refs))
    mu = jnp.mean(v, axis=-1, keepdims=True)
    vc = v - mu
    vn = vc * lax.rsqrt(jnp.mean(vc * vc, axis=-1, keepdims=True) + EPS) * lg_ref[...] + lb_ref[...]
    if want_vn:
        refs[2 * parts + 5][...] = vn
    r = lax.broadcasted_iota(jnp.int32, (CHUNK, CHUNK), 0)
    c = lax.broadcasted_iota(jnp.int32, (CHUNK, CHUNK), 1)
    keep = (c <= r) & ((r // block_rows) == (c // block_rows))
    ch = vn.shape[1] // SG_GROUPS
    for g in range(SG_GROUPS):
        w = jnp.where(keep, w_ref[g], 0.0).astype(BF16)
        mixed = jnp.dot(w, vn[:, g * ch:(g + 1) * ch].astype(BF16), preferred_element_type=F32) + b_ref[:, g:g + 1]
        sg_ref[:, g * ch:(g + 1) * ch] = (u[:, g * ch:(g + 1) * ch] * mixed).astype(sg_ref.dtype)


def _sgu(proj, su_col, sv_col, d_sg, ln_g, ln_b, w, bias_t, block_rows, want_vn):
    m = proj.shape[0]
    out_shape = [jax.ShapeDtypeStruct((m, d_sg), BF16)]
    out_specs = [pl.BlockSpec((CHUNK, d_sg), lambda i: (i, 0))]
    if want_vn:
        out_shape.append(jax.ShapeDtypeStruct((m, d_sg), F32))
        out_specs.append(pl.BlockSpec((CHUNK, d_sg), lambda i: (i, 0)))
    unit = math.gcd(math.gcd(su_col, sv_col), d_sg)
    parts = d_sg // unit
    col_specs = [pl.BlockSpec((CHUNK, unit), lambda i, cb=c0 // unit + k: (i, cb))
                 for c0 in (su_col, sv_col) for k in range(parts)]
    return pl.pallas_call(
        functools.partial(_sgu_kernel, block_rows=block_rows, parts=parts, want_vn=want_vn),
        grid=(m // CHUNK,),
        in_specs=col_specs + [pl.BlockSpec((1, d_sg), lambda i: (0, 0)),
                              pl.BlockSpec((1, d_sg), lambda i: (0, 0)),
                              pl.BlockSpec((SG_GROUPS, CHUNK, CHUNK), lambda i: (0, 0, 0)),
                              pl.BlockSpec((CHUNK, SG_GROUPS), lambda i: (0, 0))],
        out_specs=out_specs,
        out_shape=out_shape,
        compiler_params=_params("parallel"),
        name="sgu",
    )(*([proj] * (2 * parts)), ln_g.reshape(1, d_sg), ln_b.reshape(1, d_sg), w, bias_t)


def _take_top(s, key_id, n_take):
    rows, cols = s.shape
    slot = lax.broadcasted_iota(jnp.int32, (n_take, cols), 0)
    rank = jnp.full((rows, cols), float(n_take), F32)
    top = jnp.zeros((n_take, cols), F32)
    for r in range(n_take):
        m = jnp.max(s, axis=0, keepdims=True)
        first = jnp.min(jnp.where(s == m, key_id, float(rows)), axis=0, keepdims=True)
        sel = key_id == first
        rank = jnp.where(sel, float(r), rank)
        s = jnp.where(sel, -jnp.inf, s)
        top = jnp.where(slot == r, m, top)
    return top, rank


MARK = 2.0 ** 100
MARK_STEP = 2.0 ** 93
MARK_FLOOR = -(2.0 ** 90)


def _take_top_unique(s, n_take):
    cols = s.shape[1]
    slot = lax.broadcasted_iota(jnp.int32, (n_take, cols), 0)
    top = jnp.zeros((n_take, cols), F32)
    for r in range(n_take):
        m = jnp.max(s, axis=0, keepdims=True)
        top = jnp.where(slot == r, m, top)
        s = jnp.where(s == m, -(MARK + r * MARK_STEP), s)
    removed = s <= -MARK
    rank = jnp.where(removed, (-s - MARK) * (1.0 / MARK_STEP), float(n_take))
    return top, rank, jnp.sum(jnp.where(removed, 1.0, 0.0), axis=0, keepdims=True)


def _candidate_pieces(top0, top1):
    tt = top0.shape[1]
    sub = lax.broadcasted_iota(jnp.int32, (8, tt), 0)
    subf = sub.astype(F32)
    pieces, flat = [], []
    for a in range(8):
        n_b = PEER_TOPK // (a + 1)
        for b0 in range(0, n_b, 8):
            c = top0[a:a + 1, :] + top1[b0:b0 + 8, :]
            pieces.append(jnp.where(sub < n_b - b0, c, -jnp.inf))
            flat.append(subf + float(a * PEER_TOPK + b0))
    pieces.append(top0[8:16, :] + top1[0:1, :])
    flat.append((subf + 8.0) * float(PEER_TOPK))
    return pieces, flat


def _write_route(scores, top0, top1, rank0, rank1, pieces, taken, rk_ref, e1_ref, lk_ref, e0_ref):
    nk, tt = rank0.shape
    best = top0[0:1, :] + top1[0:1, :]
    z = functools.reduce(jnp.add, [jnp.sum(jnp.where(t > 0.0, jnp.exp(c - best), 0.0), axis=0, keepdims=True)
                                   for t, c in zip(taken, pieces)])
    counts = [jnp.sum(taken[0] + taken[1], axis=0, keepdims=True)]
    counts += [jnp.sum(taken[a + 1], axis=0, keepdims=True) for a in range(1, 8)]
    counts += [taken[-1][a:a + 1, :] for a in range(8)]
    lk = jnp.zeros((nk, tt), F32)
    for a in range(PEER_TOPK):
        lk = jnp.where(rank0 == float(a), counts[a], lk)
    rk_ref[...] = rank1
    e1_ref[...] = jnp.exp(scores[1] - top1[0:1, :])
    lk_ref[...] = lk
    e0_ref[...] = jnp.exp(scores[0] - top0[0:1, :]) / z


def _peer_route_kernel(q_ref, sk_ref, rk_ref, e1_ref, lk_ref, e0_ref):
    tt = q_ref.shape[0]
    half = q_ref.shape[1] // 2
    nk = sk_ref.shape[1]
    out_refs = (rk_ref, e1_ref, lk_ref, e0_ref)
    scores = [lax.dot_general(sk_ref[p].astype(BF16), q_ref[:, p * half:(p + 1) * half].astype(BF16),
                              NT_DIMS, preferred_element_type=F32) for p in range(2)]

    top0, rank0, removed0 = _take_top_unique(scores[0], PEER_TOPK)
    top1, rank1, removed1 = _take_top_unique(scores[1], PEER_TOPK)
    pieces, _ = _candidate_pieces(top0, top1)
    cand = list(pieces)
    for _ in range(PEER_TOPK):
        m = jnp.max(functools.reduce(jnp.maximum, cand), axis=0, keepdims=True)
        cand = [jnp.where(c == m, -MARK, c) for c in cand]
    taken = [jnp.where(c == -MARK, 1.0, 0.0) for c in cand]
    n_taken = functools.reduce(jnp.add, [jnp.sum(t, axis=0, keepdims=True) for t in taken])
    lowest = jnp.minimum(jnp.min(scores[0], axis=0, keepdims=True), jnp.min(scores[1], axis=0, keepdims=True))
    k = float(PEER_TOPK)
    unsure = (removed0 != k) | (removed1 != k) | (n_taken != k) | (lowest < MARK_FLOOR)
    _write_route(scores, top0, top1, rank0, rank1, pieces, taken, *out_refs)

    @pl.when(jnp.sum(jnp.where(unsure, 1.0, 0.0)) > 0.0)
    def _():
        key_id = lax.broadcasted_iota(jnp.int32, (nk, tt), 0).astype(F32)
        xtop0, xrank0 = _take_top(scores[0], key_id, PEER_TOPK)
        xtop1, xrank1 = _take_top(scores[1], key_id, PEER_TOPK)
        xpieces, flat = _candidate_pieces(xtop0, xtop1)
        xcand = list(xpieces)
        xtaken = [jnp.zeros((8, tt), F32) for _ in xpieces]
        big = float(PEER_TOPK * PEER_TOPK)
        for _ in range(PEER_TOPK):
            m = jnp.max(functools.reduce(jnp.maximum, xcand), axis=0, keepdims=True)
            first = jnp.min(functools.reduce(jnp.minimum,
                                             [jnp.where(c == m, f, big) for c, f in zip(xcand, flat)]),
                            axis=0, keepdims=True)
            for i, f in enumerate(flat):
                sel = f == first
                xcand[i] = jnp.where(sel, -jnp.inf, xcand[i])
                xtaken[i] = jnp.where(sel, 1.0, xtaken[i])
        _write_route(scores, xtop0, xtop1, xrank0, xrank1, xpieces, xtaken, *out_refs)


def _peer_route(qp, sub_keys, tt):
    m = qp.shape[0]
    nh, _, nk, half = sub_keys.shape
    shp = jax.ShapeDtypeStruct((nh, nk, m), F32)
    ospec = pl.BlockSpec((None, nk, tt), lambda i, h: (h, 0, i))
    return pl.pallas_call(
        _peer_route_kernel,
        grid=(m // tt, nh),
        in_specs=[pl.BlockSpec((tt, 2 * half), lambda i, h: (i, h)),
                  pl.BlockSpec((None, 2, nk, half), lambda i, h: (h, 0, 0, 0))],
        out_specs=[ospec] * 4,
        out_shape=[shp] * 4,
        compiler_params=_params("parallel", "arbitrary"),
        name="peer_route",
    )(qp, sub_keys)


MXU_COLS = 256


def _peer_dense_kernel(h_ref, u_ref, v_ref, rk_ref, e1_ref, lk_ref, e0_ref, o_ref, act_ref, g_ref, *,
                       n_chunks, n_items):
    s = pl.program_id(0)
    _, tt, ec = act_ref.shape
    nk = rk_ref.shape[1]
    d = v_ref.shape[1]
    first_key = (jnp.clip(s - 1, 0, n_items - 1) % n_chunks) * (ec // nk)
    down_chunk = jnp.clip(s - 2, 0, n_items - 1) % n_chunks

    @pl.when(s == 0)
    def _():
        act_ref[...] = jnp.zeros(act_ref.shape, act_ref.dtype)
        g_ref[...] = jnp.zeros(g_ref.shape, g_ref.dtype)

    @pl.when(down_chunk == 0)
    def _():
        o_ref[...] = jnp.zeros(o_ref.shape, o_ref.dtype)

    n_tiles = d // MXU_COLS
    pieces = [(ii, tb) for ii in range(ec // nk) for tb in range(tt // nk)]
    per_tile = -(-len(pieces) // n_tiles)
    half = ec // 2

    def step(rs, ws):
        key_rows = {}

        def key_row(ref, hd, ii):
            if (id(ref), hd, ii) not in key_rows:
                key_rows[id(ref), hd, ii] = ref[hd, pl.ds(first_key + ii, 1), :]
            return key_rows[id(ref), hd, ii]

        for n in range(n_tiles):
            if n == 0:
                act_ref[ws, :, :half] = lax.dot_general(h_ref[...], u_ref[:half, :], NT_DIMS,
                                                        preferred_element_type=F32)
            if n == n_tiles // 2:
                act_ref[ws, :, half:] = lax.dot_general(h_ref[...], u_ref[half:, :], NT_DIMS,
                                                        preferred_element_type=F32)
            cols = slice(n * MXU_COLS, (n + 1) * MXU_COLS)
            o_ref[:, cols] += jnp.dot(g_ref[rs], v_ref[:, cols], preferred_element_type=F32)
            for ii, tb in pieces[n * per_tile:(n + 1) * per_tile]:
                toks = slice(tb * nk, (tb + 1) * nk)
                exps = slice(ii * nk, (ii + 1) * nk)
                w = None
                for hd in range(rk_ref.shape[0]):
                    take = rk_ref[hd, :, toks] < key_row(lk_ref, hd, ii)[:, toks]
                    term = jnp.where(take, e1_ref[hd, :, toks], 0.0) * key_row(e0_ref, hd, ii)[:, toks]
                    w = term if w is None else w + term
                g_ref[ws, toks, exps] = (w.T * _gelu(act_ref[rs, toks, exps])).astype(BF16)

    @pl.when(s % 2 == 0)
    def _():
        step(1, 0)

    @pl.when(s % 2 == 1)
    def _():
        step(0, 1)


def _peer_dense(h2, u_bf, v_bf, route, tt, ec):
    m, d = h2.shape
    n_chunks = u_bf.shape[0] // ec
    n_items = (m // tt) * n_chunks
    nh, nk, _ = route[0].shape

    def up_item(s):
        return jnp.minimum(s, n_items - 1)

    def gate_item(s):
        return jnp.clip(s - 1, 0, n_items - 1)

    def down_item(s):
        return jnp.clip(s - 2, 0, n_items - 1)

    rspec = pl.BlockSpec((nh, nk, tt), lambda s: (0, 0, gate_item(s) // n_chunks), pipeline_mode=pl.Buffered(1))
    return pl.pallas_call(
        functools.partial(_peer_dense_kernel, n_chunks=n_chunks, n_items=n_items),
        grid=(n_items + 2,),
        in_specs=[pl.BlockSpec((tt, d), lambda s: (up_item(s) // n_chunks, 0), pipeline_mode=pl.Buffered(1)),
                  pl.BlockSpec((ec, d), lambda s: (up_item(s) % n_chunks, 0)),
                  pl.BlockSpec((ec, d), lambda s: (down_item(s) % n_chunks, 0)),
                  rspec, rspec, rspec, rspec],
        out_specs=pl.BlockSpec((tt, d), lambda s: (down_item(s) // n_chunks, 0)),
        out_shape=jax.ShapeDtypeStruct((m, d), F32),
        scratch_shapes=[pltpu.VMEM((2, tt, ec), F32), pltpu.VMEM((2, tt, ec), BF16)],
        compiler_params=_params("arbitrary"),
        name="peer_dense",
    )(h2, u_bf, v_bf, *route)


def _final_kernel(x_ref, p_ref, gt_ref, g_ref, o_ref):
    x = x_ref[...] + gt_ref[...] * p_ref[...]
    o_ref[...] = x * lax.rsqrt(jnp.mean(x * x, axis=-1, keepdims=True) + EPS) * g_ref[...]


def _final(x, peer, mod, gt_idx, g, rows, tm):
    m, d = x.shape
    return pl.pallas_call(
        _final_kernel,
        grid=(m // tm, 1),
        in_specs=[pl.BlockSpec((tm, d), lambda i, j: (i, 0)),
                  pl.BlockSpec((tm, d), lambda i, j: (i, 0)),
                  rows.spec(tm, d, gt_idx),
                  pl.BlockSpec((1, d), lambda i, j: (0, 0))],
        out_specs=pl.BlockSpec((tm, d), lambda i, j: (i, 0)),
        out_shape=jax.ShapeDtypeStruct((m, d), F32),
        compiler_params=_params("parallel", "arbitrary"),
        name="final",
    )(x, peer, mod, g.reshape(1, d))


def _tile(n, want):
    t = min(n, want)
    assert n % t == 0, (n, want)
    return t


def _layer(x, mod, rows, attend, sgu_block_rows, want_vn, p):
    m, d = x.shape
    d_sg = d // 2
    att_w = N_HEADS_ATT * HEAD_DIM
    tm = _tile(m, 1024)
    tm_row = _tile(m, 256)
    h1 = _normmod(x, p["norm1_g"], mod, 1, 0, rows, tm_row)
    if "peer_u_bf" in p:
        proj = _matmul(h1, p["w_in"], tm, 512, "inproj")
    else:
        proj, p["peer_u_bf"], p["peer_v_bf"] = _matmul_and_cast(h1, p["w_in"], (p["peer_u"], p["peer_v"]),
                                                               tm, 512, "inproj")
    att = attend(proj)
    sg_out = _sgu(proj, 3 * att_w, 3 * att_w + d_sg, d_sg, p["sg_ln_g"], p["sg_ln_b"],
                  p["sg_w_eff"][sgu_block_rows], p["sg_b_eff"][sgu_block_rows], sgu_block_rows, want_vn)
    sg = sg_out[0]
    merged = _merge(att.astype(BF16), sg, proj, 3 * att_w + 2 * d_sg, 3 * att_w + 2 * d_sg + d,
                    p["w_pa"], p["w_pb"], tm, 512)
    x1 = _outproj(merged, p["w_o"], x, mod, 2, rows, tm, 512)
    h2 = _normmod(x1, p["norm2_g"], mod, 4, 3, rows, tm_row)
    qp = _matmul(h2, p["peer_wq"], tm, 512, "peer_q")
    tt = _tile(m, 512)
    route = _peer_route(qp, p["peer_subkeys"], tt)
    peer = _peer_dense(h2, p["peer_u_bf"], p["peer_v_bf"], route, tt, _tile(p["peer_u_bf"].shape[0], 512))
    y = _final(x1, peer, mod, 5, p["final_g"], rows, tm_row)
    return y, proj, (sg_out[1] if want_vn else None)


def kernel(x_prompt, x_sample, cache_k0, cache_v0, cache_k1, cache_v1, cache_k2, cache_v2, c_prompt, c_sample,
           ada_w, ada_b, norm1_g, norm2_g, w_in, sg_ln_g, sg_ln_b, sg_w, sg_b, w_pa, w_pb, w_o, peer_wq,
           peer_subkeys, peer_u, peer_v, final_g):
    batch, seq, d = x_prompt.shape
    dec_batch, dec_seq, _ = x_sample.shape
    depth = ada_w.shape[0]
    assert depth == 1, "single-layer step"
    att_w = N_HEADS_ATT * HEAD_DIM
    hw = HEADS_PER_GROUP * HEAD_DIM

    slopes = jnp.exp2(-8.0 * jnp.arange(1, N_HEADS_ATT + 1, dtype=F32) / N_HEADS_ATT).reshape(N_GROUPS, HEADS_PER_GROUP)
    sg_wl, sg_bl = sg_w[0], sg_b[0]
    reps = CHUNK // dec_seq
    p = dict(norm1_g=norm1_g[0], norm2_g=norm2_g[0], w_in=w_in[0], sg_ln_g=sg_ln_g[0], sg_ln_b=sg_ln_b[0],
             w_pa=w_pa[0], w_pb=w_pb[0], w_o=w_o[0], peer_wq=peer_wq[0], peer_subkeys=peer_subkeys[0],
             peer_u=peer_u[0], peer_v=peer_v[0], final_g=final_g,
             sg_w_eff={CHUNK: sg_wl, dec_seq: jnp.tile(sg_wl[:, :dec_seq, :dec_seq], (1, reps, reps))},
             sg_b_eff={CHUNK: sg_bl.T, dec_seq: jnp.tile(sg_bl.T[:dec_seq], (reps, 1))})

    c_all = jnp.concatenate([c_prompt, c_sample], axis=0)
    mod = _modulation(c_all, ada_w[0], ada_b[0])
    mod_p = mod[:batch].reshape(batch, 1, N_MOD * d)
    mod_s = jnp.repeat(mod[batch:], dec_seq, axis=0)

    y_p, proj_p, _ = _layer(
        x_prompt.reshape(batch * seq, d), mod_p, _Rows(True, seq),
        lambda proj: _attn_prompt(proj, slopes, batch, seq), CHUNK, False, p)

    caches = [(cache_k0, cache_v0), (cache_k1, cache_v1), (cache_k2, cache_v2)]
    y_s, proj_s, vn_s = _layer(
        x_sample.reshape(dec_batch * dec_seq, d), mod_s, _Rows(False, dec_seq),
        lambda proj: _attn_sample(proj, slopes, caches, dec_batch, dec_seq), dec_seq, True, p)

    outs = [y_p.reshape(batch, seq, d), y_s.reshape(dec_batch, dec_seq, d)]
    kp = proj_p.reshape(batch, seq, -1)
    ks = proj_s.reshape(dec_batch, dec_seq, -1)
    for g in range(N_GROUPS):
        keep = min(WINDOWS[g], seq)
        for src, rows_from in ((kp, seq - keep), (ks, 0)):
            for part in (1, 2):
                c0 = part * att_w + g * hw
                blk = src[:, rows_from:, c0:c0 + hw]
                outs.append(blk.reshape(1, blk.shape[0], blk.shape[1], HEADS_PER_GROUP, HEAD_DIM))
    outs.append(vn_s.reshape(1, dec_batch, dec_seq, d // 2))
    return tuple(outs)
```

```python
import functools
import math

import jax
import jax.numpy as jnp
from jax import lax
from jax.experimental import pallas as pl
from jax.experimental.pallas import tpu as pltpu

F32 = jnp.float32
BF16 = jnp.bfloat16
EPS = 1e-6
GATE_DTYPE = jnp.float32

VMEM_LIMIT_BYTES = 58 * 1024 * 1024
LANES = 128

HEAD_DIM = 128
N_GROUPS = 3
HEADS_PER_GROUP = 8
N_HEADS_ATT = N_GROUPS * HEADS_PER_GROUP
WINDOWS = (128, 512, 2048)
DILATIONS = (1, 4, 16)
ATT_SPAN = 128
CHUNK = 128
SG_GROUPS = 8
PEER_HEADS = 8
PEER_NKEYS = 128
PEER_TOPK = 16
N_MOD = 6

NT_DIMS = (((1,), (1,)), ((), ()))


def _params(*sem):
    return pltpu.CompilerParams(dimension_semantics=sem, vmem_limit_bytes=VMEM_LIMIT_BYTES)


def _gelu(x):
    return x * (lax.erf(x * (1.0 / math.sqrt(2.0))) + 1.0) * 0.5


class _Rows:
    def __init__(self, shared, rows_per_batch):
        self.shared = shared
        self.rows_per_batch = rows_per_batch

    def spec(self, tm, tn, col_block0=0):
        if self.shared:
            tpb = self.rows_per_batch // tm
            return pl.BlockSpec((None, 1, tn), lambda i, j: (i // tpb, 0, j + col_block0))
        return pl.BlockSpec((tm, tn), lambda i, j: (i, j + col_block0))


def _mod_kernel(c_ref, w_ref, b_ref, o_ref):
    c = c_ref[...]
    a = (c * jax.nn.sigmoid(c)).astype(BF16)
    o_ref[...] = jnp.dot(a, w_ref[...].astype(BF16), preferred_element_type=F32) + b_ref[...]


def _modulation(c, w, b, tn=1024):
    n, d = c.shape
    cols = w.shape[1]
    return pl.pallas_call(
        _mod_kernel,
        grid=(cols // tn,),
        in_specs=[pl.BlockSpec((n, d), lambda j: (0, 0)),
                  pl.BlockSpec((d, tn), lambda j: (0, j)),
                  pl.BlockSpec((1, tn), lambda j: (0, j))],
        out_specs=pl.BlockSpec((n, tn), lambda j: (0, j)),
        out_shape=jax.ShapeDtypeStruct((n, cols), F32),
        compiler_params=_params("parallel"),
        name="modulation",
    )(c, w, b.reshape(1, cols))


def _normmod_kernel(x_ref, g_ref, sc_ref, sh_ref, o_ref):
    x = x_ref[...]
    y = x * lax.rsqrt(jnp.mean(x * x, axis=-1, keepdims=True) + EPS)
    o_ref[...] = ((y * g_ref[...]) * (1.0 + sc_ref[...]) + sh_ref[...]).astype(o_ref.dtype)


def _normmod(x, g, mod, sc_idx, sh_idx, rows, tm):
    m, d = x.shape
    return pl.pallas_call(
        _normmod_kernel,
        grid=(m // tm, 1),
        in_specs=[pl.BlockSpec((tm, d), lambda i, j: (i, 0)),
                  pl.BlockSpec((1, d), lambda i, j: (0, 0)),
                  rows.spec(tm, d, sc_idx),
                  rows.spec(tm, d, sh_idx)],
        out_specs=pl.BlockSpec((tm, d), lambda i, j: (i, 0)),
        out_shape=jax.ShapeDtypeStruct((m, d), BF16),
        compiler_params=_params("parallel", "arbitrary"),
        name="normmod",
    )(x, g.reshape(1, d), mod, mod)


def _mm_kernel(a_ref, w_ref, o_ref):
    o_ref[...] = jnp.dot(a_ref[...], w_ref[...].astype(BF16), preferred_element_type=F32).astype(o_ref.dtype)


def _matmul(a, w, tm, tn, name):
    m, k = a.shape
    n = w.shape[1]
    return pl.pallas_call(
        _mm_kernel,
        grid=(m // tm, n // tn),
        in_specs=[pl.BlockSpec((tm, k), lambda i, j: (i, 0)),
                  pl.BlockSpec((k, tn), lambda i, j: (0, j))],
        out_specs=pl.BlockSpec((tm, tn), lambda i, j: (i, j)),
        out_shape=jax.ShapeDtypeStruct((m, n), F32),
        compiler_params=_params("parallel", "arbitrary"),
        name=name,
    )(a, w)


def _merge_kernel(att_ref, sg_ref, ga_ref, gb_ref, wa_ref, wb_ref, o_ref):
    pa = jnp.dot(att_ref[...], wa_ref[...].astype(BF16), preferred_element_type=F32)
    pb = jnp.dot(sg_ref[...], wb_ref[...].astype(BF16), preferred_element_type=F32)
    o_ref[...] = (jax.nn.sigmoid(ga_ref[...]) * pa + jax.nn.sigmoid(gb_ref[...]) * pb).astype(o_ref.dtype)


def _merge(att, sg, proj, ga_col, gb_col, w_pa, w_pb, tm, tn):
    m, ka = att.shape
    kb = sg.shape[1]
    n = w_pa.shape[1]
    ga0, gb0 = ga_col // tn, gb_col // tn
    return pl.pallas_call(
        _merge_kernel,
        grid=(m // tm, n // tn),
        in_specs=[pl.BlockSpec((tm, ka), lambda i, j: (i, 0)),
                  pl.BlockSpec((tm, kb), lambda i, j: (i, 0)),
                  pl.BlockSpec((tm, tn), lambda i, j: (i, j + ga0)),
                  pl.BlockSpec((tm, tn), lambda i, j: (i, j + gb0)),
                  pl.BlockSpec((ka, tn), lambda i, j: (0, j)),
                  pl.BlockSpec((kb, tn), lambda i, j: (0, j))],
        out_specs=pl.BlockSpec((tm, tn), lambda i, j: (i, j)),
        out_shape=jax.ShapeDtypeStruct((m, n), BF16),
        compiler_params=_params("parallel", "arbitrary"),
        name="merge",
    )(att, sg, proj, proj, w_pa, w_pb)


def _outproj_kernel(a_ref, w_ref, x_ref, gt_ref, o_ref):
    acc = jnp.dot(a_ref[...], w_ref[...].astype(BF16), preferred_element_type=F32)
    o_ref[...] = x_ref[...] + gt_ref[...] * acc


def _outproj(a, w, x, mod, gt_idx, rows, tm, tn):
    m, k = a.shape
    n = w.shape[1]
    return pl.pallas_call(
        _outproj_kernel,
        grid=(m // tm, n // tn),
        in_specs=[pl.BlockSpec((tm, k), lambda i, j: (i, 0)),
                  pl.BlockSpec((k, tn), lambda i, j: (0, j)),
                  pl.BlockSpec((tm, tn), lambda i, j: (i, j)),
                  rows.spec(tm, tn, gt_idx * (n // tn))],
        out_specs=pl.BlockSpec((tm, tn), lambda i, j: (i, j)),
        out_shape=jax.ShapeDtypeStruct((m, n), F32),
        compiler_params=_params("parallel", "arbitrary"),
        name="outproj",
    )(a, w, x, mod)


def _softmax_parts(s):
    m = jnp.max(s, axis=-1, keepdims=True)
    p = jnp.exp(s - m)
    return m, p, jnp.sum(p, axis=-1, keepdims=True)


def _combine_groups(outs, lses):
    top = functools.reduce(jnp.maximum, lses)
    es = [jnp.exp(l - top) for l in lses]
    tot = functools.reduce(jnp.add, es)
    return functools.reduce(jnp.add, [(e / tot) * o for e, o in zip(es, outs)])


def _attn_prompt_kernel(sl_ref, *refs, seq):
    q_refs, k_refs, v_refs = refs[0:9:3], refs[1:9:3], refs[2:9:3]
    o_ref, og_ref, lg_ref = refs[9], refs[10], refs[11]
    h = pl.program_id(1)
    scale = HEAD_DIM ** -0.5
    span = ATT_SPAN
    back1 = lax.broadcasted_iota(jnp.int32, (span, span), 0) - lax.broadcasted_iota(jnp.int32, (span, span), 1)
    back2 = (lax.broadcasted_iota(jnp.int32, (span, 2 * span), 0) + span
             - lax.broadcasted_iota(jnp.int32, (span, 2 * span), 1))
    for g in range(N_GROUPS):
        dil = DILATIONS[g]
        n_blocks = seq // dil // span
        step_bias = sl_ref[g, h] * float(dil)
        bias1 = jnp.where(back1 >= 0, -(step_bias * back1.astype(F32)), -jnp.inf)
        bias2 = jnp.where((back2 >= 0) & (back2 <= span), -(step_bias * back2.astype(F32)), -jnp.inf)
        blocks = [(res, blk) for blk in range(n_blocks) for res in range(dil)]
        n_first = dil
        rows = [pl.ds(res + dil * span * blk, span) if dil == 1 else pl.ds(res + dil * span * blk, span, stride=dil)
                for res, blk in blocks]
        q = jnp.stack([q_refs[g][r, :] for r in rows]).astype(BF16)
        k_cur = [k_refs[g][r, :].astype(BF16) for r in rows]
        v_cur = [v_refs[g][r, :].astype(BF16) for r in rows]
        if n_blocks == 1:
            k, v = jnp.stack(k_cur), jnp.stack(v_cur)
            bias = bias1[None]
        else:
            prev = [i if blk == 0 else blocks.index((res, blk - 1)) for i, (res, blk) in enumerate(blocks)]
            k = jnp.stack([jnp.concatenate([k_cur[j], k_cur[i]], axis=0) for i, j in enumerate(prev)])
            v = jnp.stack([jnp.concatenate([v_cur[j], v_cur[i]], axis=0) for i, j in enumerate(prev)])
            bias_first = jnp.concatenate([jnp.full((span, span), -jnp.inf, F32), bias1], axis=1)
            bias = jnp.concatenate([jnp.broadcast_to(bias_first[None], (n_first, span, 2 * span)),
                                    jnp.broadcast_to(bias2[None], (len(blocks) - n_first, span, 2 * span))], axis=0)
        s = lax.dot_general(q, k, (((2,), (2,)), ((0,), (0,))), preferred_element_type=F32) * scale + bias
        m, p, den = _softmax_parts(s)
        o = lax.dot_general((p / den).astype(BF16), v, (((2,), (1,)), ((0,), (0,))), preferred_element_type=F32)
        lse = m + jnp.log(den)
        for i, r in enumerate(rows):
            og_ref[g, r, :] = o[i]
            lg_ref[g, r, :] = jnp.broadcast_to(lse[i], (span, HEAD_DIM))

    def combine(i, carry):
        rows = pl.ds(pl.multiple_of(i * span, span), span)
        o_ref[rows, :] = _combine_groups([og_ref[g, rows, :] for g in range(N_GROUPS)],
                                         [lg_ref[g, rows, :] for g in range(N_GROUPS)]).astype(o_ref.dtype)
        return carry

    lax.fori_loop(0, seq // span, combine, 0)


def _attn_prompt(qkv, slopes, batch, seq):
    ncols = qkv.shape[1]
    assert seq % (max(DILATIONS) * ATT_SPAN) == 0, "every residue class needs whole query blocks"
    qkv3 = qkv.reshape(batch, seq, ncols)
    in_specs = [pl.BlockSpec(memory_space=pltpu.SMEM)]
    for g in range(N_GROUPS):
        for part in range(3):
            cb = part * N_HEADS_ATT + g * HEADS_PER_GROUP
            in_specs.append(pl.BlockSpec((None, seq, HEAD_DIM), lambda b, h, cb=cb: (b, 0, cb + h)))
    out = pl.pallas_call(
        functools.partial(_attn_prompt_kernel, seq=seq),
        grid=(batch, HEADS_PER_GROUP),
        in_specs=in_specs,
        out_specs=pl.BlockSpec((None, seq, HEAD_DIM), lambda b, h: (b, 0, h)),
        out_shape=jax.ShapeDtypeStruct((batch, seq, HEADS_PER_GROUP * HEAD_DIM), BF16),
        scratch_shapes=[pltpu.VMEM((N_GROUPS, seq, HEAD_DIM), F32),
                        pltpu.VMEM((N_GROUPS, seq, HEAD_DIM), F32)],
        compiler_params=_params("parallel", "arbitrary"),
        name="attn_prompt",
    )(slopes, *([qkv3] * 9))
    return out.reshape(batch * seq, HEADS_PER_GROUP * HEAD_DIM)


def _attn_sample_kernel(sl_ref, qkv_ref, k0, v0, k1, v1, k2, v2, o_ref, *, dec_seq):
    scale = HEAD_DIM ** -0.5
    att_w = N_HEADS_ATT * HEAD_DIM
    nh = HEADS_PER_GROUP
    caches = ((k0, v0), (k1, v1), (k2, v2))
    batch_nt = (((2,), (2,)), ((0,), (0,)))
    batch_nn = (((2,), (1,)), ((0,), (0,)))
    t_new = lax.broadcasted_iota(jnp.int32, (dec_seq, dec_seq), 0)
    c_new = lax.broadcasted_iota(jnp.int32, (dec_seq, dec_seq), 1)
    outs, lses = [], []
    for g in range(N_GROUPS):
        dil, win = DILATIONS[g], WINDOWS[g]

        def new_rows(part):
            cols = [part * att_w + (g * nh + h) * HEAD_DIM for h in range(nh)]
            return jnp.stack([qkv_ref[:, c:c + HEAD_DIM] for c in cols]).astype(BF16)

        def cached(ref):
            per_head = []
            for h in range(nh):
                head_rows = pl.ds(h, ref.shape[-2] // nh, stride=nh)
                per_head.append(ref[:, head_rows, :].reshape(-1, HEAD_DIM) if g == 2 else ref[head_rows, :])
            return jnp.stack(per_head).astype(BF16)

        q, kn, vn = new_rows(0), new_rows(1), new_rows(2)
        kc, vc = cached(caches[g][0]), cached(caches[g][1])
        rows = kc.shape[1]
        t_c = lax.broadcasted_iota(jnp.int32, (dec_seq, rows), 0)
        c_c = lax.broadcasted_iota(jnp.int32, (dec_seq, rows), 1)
        e_c = ((c_c >> 3) << 4) + (c_c & 7) if g == 2 else c_c
        d_c = win + t_c - e_c
        ok_c = ((d_c & (dil - 1)) == 0) & (d_c <= ATT_SPAN * dil)
        d_n = t_new - c_new
        ok_n = (d_n >= 0) & ((d_n & (dil - 1)) == 0)
        slope = jnp.stack([jnp.full((1, 1), sl_ref[g, h], F32) for h in range(nh)])
        s_c = lax.dot_general(q, kc, batch_nt, preferred_element_type=F32) * scale
        s_n = lax.dot_general(q, kn, batch_nt, preferred_element_type=F32) * scale
        s_c = jnp.where(ok_c[None], s_c - slope * d_c.astype(F32)[None], -jnp.inf)
        s_n = jnp.where(ok_n[None], s_n - slope * d_n.astype(F32)[None], -jnp.inf)
        m = jnp.maximum(jnp.max(s_c, axis=-1, keepdims=True), jnp.max(s_n, axis=-1, keepdims=True))
        p_c = jnp.exp(s_c - m)
        p_n = jnp.exp(s_n - m)
        den = jnp.sum(p_c, axis=-1, keepdims=True) + jnp.sum(p_n, axis=-1, keepdims=True)
        outs.append(lax.dot_general((p_c / den).astype(BF16), vc, batch_nn, preferred_element_type=F32)
                    + lax.dot_general((p_n / den).astype(BF16), vn, batch_nn, preferred_element_type=F32))
        lses.append(m + jnp.log(den))
    combined = _combine_groups(outs, lses)
    for h in range(nh):
        o_ref[:, h * HEAD_DIM:(h + 1) * HEAD_DIM] = combined[h]


def _attn_sample(qkv, slopes, caches, dec_batch, dec_seq):
    att_w = N_HEADS_ATT * HEAD_DIM
    hw = HEADS_PER_GROUP * HEAD_DIM
    ins, specs = [], []
    for g in range(N_GROUPS):
        for c in caches[g]:
            win = c.shape[2]
            if g == 2:
                dil = DILATIONS[g]
                assert dec_seq <= dil and dec_seq % 8 == 0
                ins.append(c.reshape(1, dec_batch, win // dil, dil * HEADS_PER_GROUP, HEAD_DIM))
                specs.append(pl.BlockSpec((None, None, win // dil, dec_seq * HEADS_PER_GROUP, HEAD_DIM),
                                          lambda b: (0, b, 0, 0, 0)))
            else:
                ins.append(c.reshape(1, dec_batch, win * HEADS_PER_GROUP, HEAD_DIM))
                specs.append(pl.BlockSpec((None, None, win * HEADS_PER_GROUP, HEAD_DIM), lambda b: (0, b, 0, 0)))
    return pl.pallas_call(
        functools.partial(_attn_sample_kernel, dec_seq=dec_seq),
        grid=(dec_batch,),
        in_specs=[pl.BlockSpec(memory_space=pltpu.SMEM),
                  pl.BlockSpec((dec_seq, 3 * att_w), lambda b: (b, 0))] + specs,
        out_specs=pl.BlockSpec((dec_seq, hw), lambda b: (b, 0)),
        out_shape=jax.ShapeDtypeStruct((dec_batch * dec_seq, hw), F32),
        compiler_params=_params("parallel"),
        name="attn_sample",
    )(slopes, qkv, *ins)


def _sgu_kernel(*refs, block_rows, parts, want_vn):
    su_refs, sv_refs = refs[:parts], refs[parts:2 * parts]
    lg_ref, lb_ref, w_ref, b_ref, sg_ref = refs[2 * parts:2 * parts + 5]

    def whole(rs):
        return rs[0][...] if parts == 1 else jnp.concatenate([r[...] for r in rs], axis=1)

    u = _gelu(whole(su_refs))
    v = _gelu(whole(sv_refs))
    mu = jnp.mean(v, axis=-1, keepdims=True)
    vc = v - mu
    vn = vc * lax.rsqrt(jnp.mean(vc * vc, axis=-1, keepdims=True) + EPS) * lg_ref[...] + lb_ref[...]
    if want_vn:
        refs[2 * parts + 5][...] = vn
    r = lax.broadcasted_iota(jnp.int32, (CHUNK, CHUNK), 0)
    c = lax.broadcasted_iota(jnp.int32, (CHUNK, CHUNK), 1)
    keep = (c <= r) & ((r // block_rows) == (c // block_rows))
    ch = vn.shape[1] // SG_GROUPS
    for g in range(SG_GROUPS):
        w = jnp.where(keep, w_ref[g], 0.0).astype(BF16)
        mixed = jnp.dot(w, vn[:, g * ch:(g + 1) * ch].astype(BF16), preferred_element_type=F32) + b_ref[:, g:g + 1]
        sg_ref[:, g * ch:(g + 1) * ch] = (u[:, g * ch:(g + 1) * ch] * mixed).astype(sg_ref.dtype)


def _sgu(proj, su_col, sv_col, d_sg, ln_g, ln_b, w, bias_t, block_rows, want_vn):
    m = proj.shape[0]
    out_shape = [jax.ShapeDtypeStruct((m, d_sg), BF16)]
    out_specs = [pl.BlockSpec((CHUNK, d_sg), lambda i: (i, 0))]
    if want_vn:
        out_shape.append(jax.ShapeDtypeStruct((m, d_sg), F32))
        out_specs.append(pl.BlockSpec((CHUNK, d_sg), lambda i: (i, 0)))
    unit = math.gcd(math.gcd(su_col, sv_col), d_sg)
    parts = d_sg // unit
    col_specs = [pl.BlockSpec((CHUNK, unit), lambda i, cb=c0 // unit + k: (i, cb))
                 for c0 in (su_col, sv_col) for k in range(parts)]
    return pl.pallas_call(
        functools.partial(_sgu_kernel, block_rows=block_rows, parts=parts, want_vn=want_vn),
        grid=(m // CHUNK,),
        in_specs=col_specs + [pl.BlockSpec((1, d_sg), lambda i: (0, 0)),
                              pl.BlockSpec((1, d_sg), lambda i: (0, 0)),
                              pl.BlockSpec((SG_GROUPS, CHUNK, CHUNK), lambda i: (0, 0, 0)),
                              pl.BlockSpec((CHUNK, SG_GROUPS), lambda i: (0, 0))],
        out_specs=out_specs,
        out_shape=out_shape,
        compiler_params=_params("parallel"),
        name="sgu",
    )(*([proj] * (2 * parts)), ln_g.reshape(1, d_sg), ln_b.reshape(1, d_sg), w, bias_t)


def _take_top(s, key_id, n_take):
    rows, cols = s.shape
    slot = lax.broadcasted_iota(jnp.int32, (n_take, cols), 0)
    rank = jnp.full((rows, cols), float(n_take), F32)
    top = jnp.zeros((n_take, cols), F32)
    for r in range(n_take):
        m = jnp.max(s, axis=0, keepdims=True)
        first = jnp.min(jnp.where(s == m, key_id, float(rows)), axis=0, keepdims=True)
        sel = key_id == first
        rank = jnp.where(sel, float(r), rank)
        s = jnp.where(sel, -jnp.inf, s)
        top = jnp.where(slot == r, m, top)
    return top, rank


MARK = 2.0 ** 100
MARK_STEP = 2.0 ** 93
MARK_FLOOR = -(2.0 ** 90)


def _take_top_unique(s, n_take):
    cols = s.shape[1]
    slot = lax.broadcasted_iota(jnp.int32, (n_take, cols), 0)
    top = jnp.zeros((n_take, cols), F32)
    for r in range(n_take):
        m = jnp.max(s, axis=0, keepdims=True)
        top = jnp.where(slot == r, m, top)
        s = jnp.where(s == m, -(MARK + r * MARK_STEP), s)
    removed = s <= -MARK
    rank = jnp.where(removed, (-s - MARK) * (1.0 / MARK_STEP), float(n_take))
    return top, rank, jnp.sum(jnp.where(removed, 1.0, 0.0), axis=0, keepdims=True)


def _candidate_pieces(top0, top1):
    tt = top0.shape[1]
    sub = lax.broadcasted_iota(jnp.int32, (8, tt), 0)
    subf = sub.astype(F32)
    pieces, flat = [], []
    for a in range(8):
        n_b = PEER_TOPK // (a + 1)
        for b0 in range(0, n_b, 8):
            c = top0[a:a + 1, :] + top1[b0:b0 + 8, :]
            pieces.append(jnp.where(sub < n_b - b0, c, -jnp.inf))
            flat.append(subf + float(a * PEER_TOPK + b0))
    pieces.append(top0[8:16, :] + top1[0:1, :])
    flat.append((subf + 8.0) * float(PEER_TOPK))
    return pieces, flat


def _write_route(scores, top0, top1, rank0, rank1, pieces, taken, rk_ref, e1_ref, lk_ref, e0_ref):
    nk, tt = rank0.shape
    best = top0[0:1, :] + top1[0:1, :]
    z = functools.reduce(jnp.add, [jnp.sum(jnp.where(t > 0.0, jnp.exp(c - best), 0.0), axis=0, keepdims=True)
                                   for t, c in zip(taken, pieces)])
    counts = [jnp.sum(taken[0] + taken[1], axis=0, keepdims=True)]
    counts += [jnp.sum(taken[a + 1], axis=0, keepdims=True) for a in range(1, 8)]
    counts += [taken[-1][a:a + 1, :] for a in range(8)]
    lk = jnp.zeros((nk, tt), F32)
    for a in range(PEER_TOPK):
        lk = jnp.where(rank0 == float(a), counts[a], lk)
    rk_ref[...] = rank1.astype(rk_ref.dtype)
    e1_ref[...] = jnp.exp(scores[1] - top1[0:1, :]).astype(e1_ref.dtype)
    lk_ref[...] = lk
    e0_ref[...] = jnp.exp(scores[0] - top0[0:1, :]) / z


def _peer_route_kernel(q_ref, sk_ref, *refs):
    tt = q_ref.shape[0]
    half = q_ref.shape[1] // 2
    nk = sk_ref.shape[1]
    if len(refs) > 4:
        t0_ref, t1_ref, c0_ref, c1_ref = refs[0], refs[1], refs[6], refs[7]
        c0_ref[...] = t0_ref[...].astype(BF16)
        c1_ref[...] = t1_ref[...].astype(BF16)
        refs = refs[2:6]
    out_refs = refs
    scores = [lax.dot_general(sk_ref[p].astype(BF16), q_ref[:, p * half:(p + 1) * half].astype(BF16),
                              NT_DIMS, preferred_element_type=F32) for p in range(2)]

    top0, rank0, removed0 = _take_top_unique(scores[0], PEER_TOPK)
    top1, rank1, removed1 = _take_top_unique(scores[1], PEER_TOPK)
    pieces, _ = _candidate_pieces(top0, top1)
    cand = list(pieces)
    for _ in range(PEER_TOPK):
        m = jnp.max(functools.reduce(jnp.maximum, cand), axis=0, keepdims=True)
        cand = [jnp.where(c == m, -MARK, c) for c in cand]
    taken = [jnp.where(c == -MARK, 1.0, 0.0) for c in cand]
    n_taken = functools.reduce(jnp.add, [jnp.sum(t, axis=0, keepdims=True) for t in taken])
    lowest = jnp.minimum(jnp.min(scores[0], axis=0, keepdims=True), jnp.min(scores[1], axis=0, keepdims=True))
    k = float(PEER_TOPK)
    unsure = (removed0 != k) | (removed1 != k) | (n_taken != k) | (lowest < MARK_FLOOR)
    _write_route(scores, top0, top1, rank0, rank1, pieces, taken, *out_refs)

    @pl.when(jnp.sum(jnp.where(unsure, 1.0, 0.0)) > 0.0)
    def _():
        key_id = lax.broadcasted_iota(jnp.int32, (nk, tt), 0).astype(F32)
        xtop0, xrank0 = _take_top(scores[0], key_id, PEER_TOPK)
        xtop1, xrank1 = _take_top(scores[1], key_id, PEER_TOPK)
        xpieces, flat = _candidate_pieces(xtop0, xtop1)
        xcand = list(xpieces)
        xtaken = [jnp.zeros((8, tt), F32) for _ in xpieces]
        big = float(PEER_TOPK * PEER_TOPK)
        for _ in range(PEER_TOPK):
            m = jnp.max(functools.reduce(jnp.maximum, xcand), axis=0, keepdims=True)
            first = jnp.min(functools.reduce(jnp.minimum,
                                             [jnp.where(c == m, f, big) for c, f in zip(xcand, flat)]),
                            axis=0, keepdims=True)
            for i, f in enumerate(flat):
                sel = f == first
                xcand[i] = jnp.where(sel, -jnp.inf, xcand[i])
                xtaken[i] = jnp.where(sel, 1.0, xtaken[i])
        _write_route(scores, xtop0, xtop1, xrank0, xrank1, xpieces, xtaken, *out_refs)


def _peer_route(qp, sub_keys, tt, tables=()):
    m = qp.shape[0]
    nh, _, nk, half = sub_keys.shape
    ospec = pl.BlockSpec((None, nk, tt), lambda i, h: (h, 0, i))
    in_specs = [pl.BlockSpec((tt, 2 * half), lambda i, h: (i, h)),
                pl.BlockSpec((None, 2, nk, half), lambda i, h: (h, 0, 0, 0))]
    out_specs = [ospec] * 4
    out_shape = [jax.ShapeDtypeStruct((nh, nk, m), dt) for dt in (GATE_DTYPE, GATE_DTYPE, F32, F32)]
    if tables:
        rows, width = tables[0].shape
        n_steps = (m // tt) * nh
        rb = rows // n_steps
        assert rows % n_steps == 0 and rb % 16 == 0, (rows, n_steps)
        tspec = pl.BlockSpec((rb, width), lambda i, h: (i * nh + h, 0))
        in_specs += [tspec, tspec]
        out_specs += [tspec, tspec]
        out_shape += [jax.ShapeDtypeStruct((rows, width), BF16)] * 2
    return pl.pallas_call(
        _peer_route_kernel,
        grid=(m // tt, nh),
        in_specs=in_specs,
        out_specs=out_specs,
        out_shape=out_shape,
        compiler_params=_params("arbitrary", "arbitrary"),
        name="peer_route",
    )(qp, sub_keys, *tables)


MXU_COLS = 256


def _peer_dense_kernel(h_ref, u_ref, v_ref, rk_ref, e1_ref, lk_ref, e0_ref, o_ref, act_ref, g_ref, *,
                       n_chunks, n_items):
    s = pl.program_id(0)
    _, tt, ec = act_ref.shape
    nk = rk_ref.shape[1]
    d = v_ref.shape[1]
    first_key = (jnp.clip(s - 1, 0, n_items - 1) % n_chunks) * (ec // nk)
    down_chunk = jnp.clip(s - 2, 0, n_items - 1) % n_chunks

    @pl.when(s == 0)
    def _():
        act_ref[...] = jnp.zeros(act_ref.shape, act_ref.dtype)
        g_ref[...] = jnp.zeros(g_ref.shape, g_ref.dtype)

    @pl.when(down_chunk == 0)
    def _():
        o_ref[...] = jnp.zeros(o_ref.shape, o_ref.dtype)

    n_tiles = d // MXU_COLS
    pieces = [(ii, tb) for ii in range(ec // nk) for tb in range(tt // nk)]
    per_tile = -(-len(pieces) // n_tiles)
    half = ec // 2

    def step(rs, ws):
        key_rows = {}

        def key_row(ref, hd, ii):
            if (id(ref), hd, ii) not in key_rows:
                key_rows[id(ref), hd, ii] = ref[hd, pl.ds(first_key + ii, 1), :].astype(GATE_DTYPE)
            return key_rows[id(ref), hd, ii]

        for n in range(n_tiles):
            if n == 0:
                act_ref[ws, :, :half] = lax.dot_general(h_ref[...], u_ref[:half, :], NT_DIMS,
                                                        preferred_element_type=F32)
            if n == n_tiles // 2:
                act_ref[ws, :, half:] = lax.dot_general(h_ref[...], u_ref[half:, :], NT_DIMS,
                                                        preferred_element_type=F32)
            cols = slice(n * MXU_COLS, (n + 1) * MXU_COLS)
            o_ref[:, cols] += jnp.dot(g_ref[rs], v_ref[:, cols], preferred_element_type=F32)
            for ii, tb in pieces[n * per_tile:(n + 1) * per_tile]:
                toks = slice(tb * nk, (tb + 1) * nk)
                exps = slice(ii * nk, (ii + 1) * nk)
                w = None
                for hd in range(rk_ref.shape[0]):
                    take = rk_ref[hd, :, toks] < key_row(lk_ref, hd, ii)[:, toks]
                    term = (jnp.where(take, e1_ref[hd, :, toks], jnp.zeros((), GATE_DTYPE))
                            * key_row(e0_ref, hd, ii)[:, toks])
                    w = term if w is None else w + term
                g_ref[ws, toks, exps] = (w.astype(F32).T * _gelu(act_ref[rs, toks, exps])).astype(BF16)

    @pl.when(s % 2 == 0)
    def _():
        step(1, 0)

    @pl.when(s % 2 == 1)
    def _():
        step(0, 1)


def _peer_dense(h2, u_bf, v_bf, route, tt, ec):
    m, d = h2.shape
    n_chunks = u_bf.shape[0] // ec
    n_items = (m // tt) * n_chunks
    nh, nk, _ = route[0].shape

    def up_item(s):
        return jnp.minimum(s, n_items - 1)

    def gate_item(s):
        return jnp.clip(s - 1, 0, n_items - 1)

    def down_item(s):
        return jnp.clip(s - 2, 0, n_items - 1)

    rspec = pl.BlockSpec((nh, nk, tt), lambda s: (0, 0, gate_item(s) // n_chunks), pipeline_mode=pl.Buffered(1))
    return pl.pallas_call(
        functools.partial(_peer_dense_kernel, n_chunks=n_chunks, n_items=n_items),
        grid=(n_items + 2,),
        in_specs=[pl.BlockSpec((tt, d), lambda s: (up_item(s) // n_chunks, 0), pipeline_mode=pl.Buffered(1)),
                  pl.BlockSpec((ec, d), lambda s: (up_item(s) % n_chunks, 0)),
                  pl.BlockSpec((ec, d), lambda s: (down_item(s) % n_chunks, 0)),
                  rspec, rspec, rspec, rspec],
        out_specs=pl.BlockSpec((tt, d), lambda s: (down_item(s) // n_chunks, 0)),
        out_shape=jax.ShapeDtypeStruct((m, d), F32),
        scratch_shapes=[pltpu.VMEM((2, tt, ec), F32), pltpu.VMEM((2, tt, ec), BF16)],
        compiler_params=_params("arbitrary"),
        name="peer_dense",
    )(h2, u_bf, v_bf, *route)


def _final_kernel(x_ref, p_ref, gt_ref, g_ref, o_ref):
    x = x_ref[...] + gt_ref[...] * p_ref[...]
    o_ref[...] = x * lax.rsqrt(jnp.mean(x * x, axis=-1, keepdims=True) + EPS) * g_ref[...]


def _final(x, peer, mod, gt_idx, g, rows, tm):
    m, d = x.shape
    return pl.pallas_call(
        _final_kernel,
        grid=(m // tm, 1),
        in_specs=[pl.BlockSpec((tm, d), lambda i, j: (i, 0)),
                  pl.BlockSpec((tm, d), lambda i, j: (i, 0)),
                  rows.spec(tm, d, gt_idx),
                  pl.BlockSpec((1, d), lambda i, j: (0, 0))],
        out_specs=pl.BlockSpec((tm, d), lambda i, j: (i, 0)),
        out_shape=jax.ShapeDtypeStruct((m, d), F32),
        compiler_params=_params("parallel", "arbitrary"),
        name="final",
    )(x, peer, mod, g.reshape(1, d))


def _tile(n, want):
    t = min(n, want)
    assert n % t == 0, (n, want)
    return t


def _layer(x, mod, rows, attend, sgu_block_rows, want_vn, p):
    m, d = x.shape
    d_sg = d // 2
    att_w = N_HEADS_ATT * HEAD_DIM
    tm = _tile(m, 1024)
    tm_row = _tile(m, 256)
    h1 = _normmod(x, p["norm1_g"], mod, 1, 0, rows, tm_row)
    proj = _matmul(h1, p["w_in"], tm, 512, "inproj")
    att = attend(proj)
    sg_out = _sgu(proj, 3 * att_w, 3 * att_w + d_sg, d_sg, p["sg_ln_g"], p["sg_ln_b"],
                  p["sg_w_eff"][sgu_block_rows], p["sg_b_eff"][sgu_block_rows], sgu_block_rows, want_vn)
    sg = sg_out[0]
    merged = _merge(att.astype(BF16), sg, proj, 3 * att_w + 2 * d_sg, 3 * att_w + 2 * d_sg + d,
                    p["w_pa"], p["w_pb"], tm, 512)
    x1 = _outproj(merged, p["w_o"], x, mod, 2, rows, tm, 512)
    h2 = _normmod(x1, p["norm2_g"], mod, 4, 3, rows, tm_row)
    qp = _matmul(h2, p["peer_wq"], tm, 512, "peer_q")
    tt = _tile(m, 512)
    if "peer_u_bf" in p:
        route = _peer_route(qp, p["peer_subkeys"], tt)
    else:
        *route, p["peer_u_bf"], p["peer_v_bf"] = _peer_route(qp, p["peer_subkeys"], tt, (p["peer_u"], p["peer_v"]))
    peer = _peer_dense(h2, p["peer_u_bf"], p["peer_v_bf"], route, tt, _tile(p["peer_u_bf"].shape[0], 512))
    y = _final(x1, peer, mod, 5, p["final_g"], rows, tm_row)
    return y, proj, (sg_out[1] if want_vn else None)


def kernel(x_prompt, x_sample, cache_k0, cache_v0, cache_k1, cache_v1, cache_k2, cache_v2, c_prompt, c_sample,
           ada_w, ada_b, norm1_g, norm2_g, w_in, sg_ln_g, sg_ln_b, sg_w, sg_b, w_pa, w_pb, w_o, peer_wq,
           peer_subkeys, peer_u, peer_v, final_g):
    batch, seq, d = x_prompt.shape
    dec_batch, dec_seq, _ = x_sample.shape
    depth = ada_w.shape[0]
    assert depth == 1, "single-layer step"
    att_w = N_HEADS_ATT * HEAD_DIM
    hw = HEADS_PER_GROUP * HEAD_DIM

    slopes = jnp.exp2(-8.0 * jnp.arange(1, N_HEADS_ATT + 1, dtype=F32) / N_HEADS_ATT).reshape(N_GROUPS, HEADS_PER_GROUP)
    sg_wl, sg_bl = sg_w[0], sg_b[0]
    reps = CHUNK // dec_seq
    p = dict(norm1_g=norm1_g[0], norm2_g=norm2_g[0], w_in=w_in[0], sg_ln_g=sg_ln_g[0], sg_ln_b=sg_ln_b[0],
             w_pa=w_pa[0], w_pb=w_pb[0], w_o=w_o[0], peer_wq=peer_wq[0], peer_subkeys=peer_subkeys[0],
             peer_u=peer_u[0], peer_v=peer_v[0], final_g=final_g,
             sg_w_eff={CHUNK: sg_wl, dec_seq: jnp.tile(sg_wl[:, :dec_seq, :dec_seq], (1, reps, reps))},
             sg_b_eff={CHUNK: sg_bl.T, dec_seq: jnp.tile(sg_bl.T[:dec_seq], (reps, 1))})

    c_all = jnp.concatenate([c_prompt, c_sample], axis=0)
    mod = _modulation(c_all, ada_w[0], ada_b[0])
    mod_p = mod[:batch].reshape(batch, 1, N_MOD * d)
    mod_s = jnp.repeat(mod[batch:], dec_seq, axis=0)

    y_p, proj_p, _ = _layer(
        x_prompt.reshape(batch * seq, d), mod_p, _Rows(True, seq),
        lambda proj: _attn_prompt(proj, slopes, batch, seq), CHUNK, False, p)

    caches = [(cache_k0, cache_v0), (cache_k1, cache_v1), (cache_k2, cache_v2)]
    y_s, proj_s, vn_s = _layer(
        x_sample.reshape(dec_batch * dec_seq, d), mod_s, _Rows(False, dec_seq),
        lambda proj: _attn_sample(proj, slopes, caches, dec_batch, dec_seq), dec_seq, True, p)

    outs = [y_p.reshape(batch, seq, d), y_s.reshape(dec_batch, dec_seq, d)]
    kp = proj_p.reshape(batch, seq, -1)
    ks = proj_s.reshape(dec_batch, dec_seq, -1)
    for g in range(N_GROUPS):
        keep = min(WINDOWS[g], seq)
        for src, rows_from in ((kp, seq - keep), (ks, 0)):
            for part in (1, 2):
                c0 = part * att_w + g * hw
                blk = src[:, rows_from:, c0:c0 + hw]
                outs.append(blk.reshape(1, blk.shape[0], blk.shape[1], HEADS_PER_GROUP, HEAD_DIM))
    outs.append(vn_s.reshape(1, dec_batch, dec_seq, d // 2))
    return tuple(outs)
```

```python
import functools
import math

import jax
import jax.numpy as jnp
from jax import lax
from jax.experimental import pallas as pl
from jax.experimental.pallas import tpu as pltpu

F32 = jnp.float32
BF16 = jnp.bfloat16
EPS = 1e-6
GATE_DTYPE = jnp.float32

VMEM_LIMIT_BYTES = 58 * 1024 * 1024
LANES = 128

HEAD_DIM = 128
N_GROUPS = 3
HEADS_PER_GROUP = 8
N_HEADS_ATT = N_GROUPS * HEADS_PER_GROUP
WINDOWS = (128, 512, 2048)
DILATIONS = (1, 4, 16)
ATT_SPAN = 128
CHUNK = 128
SG_GROUPS = 8
PEER_HEADS = 8
PEER_NKEYS = 128
PEER_TOPK = 16
N_MOD = 6

NT_DIMS = (((1,), (1,)), ((), ()))


def _params(*sem):
    return pltpu.CompilerParams(dimension_semantics=sem, vmem_limit_bytes=VMEM_LIMIT_BYTES)


def _gelu(x):
    return x * (lax.erf(x * (1.0 / math.sqrt(2.0))) + 1.0) * 0.5


class _Rows:
    def __init__(self, shared, rows_per_batch):
        self.shared = shared
        self.rows_per_batch = rows_per_batch

    def spec(self, tm, tn, col_block0=0, follow_j=True):
        step = 1 if follow_j else 0
        if self.shared:
            tpb = self.rows_per_batch // tm
            return pl.BlockSpec((None, 1, tn), lambda i, j: (i // tpb, 0, j * step + col_block0))
        return pl.BlockSpec((tm, tn), lambda i, j: (i, j * step + col_block0))


def _mod_kernel(c_ref, w_ref, b_ref, o_ref):
    c = c_ref[...]
    a = (c * jax.nn.sigmoid(c)).astype(BF16)
    o_ref[...] = jnp.dot(a, w_ref[...].astype(BF16), preferred_element_type=F32) + b_ref[...]


def _modulation(c, w, b, tn=1024):
    n, d = c.shape
    cols = w.shape[1]
    return pl.pallas_call(
        _mod_kernel,
        grid=(cols // tn,),
        in_specs=[pl.BlockSpec((n, d), lambda j: (0, 0)),
                  pl.BlockSpec((d, tn), lambda j: (0, j)),
                  pl.BlockSpec((1, tn), lambda j: (0, j))],
        out_specs=pl.BlockSpec((n, tn), lambda j: (0, j)),
        out_shape=jax.ShapeDtypeStruct((n, cols), F32),
        compiler_params=_params("parallel"),
        name="modulation",
    )(c, w, b.reshape(1, cols))


def _normed_mm_kernel(x_ref, g_ref, sc_ref, sh_ref, w_ref, o_ref, *rest):
    h_ref, = rest

    @pl.when(pl.program_id(1) == 0)
    def _():
        tm = x_ref.shape[0]
        chunk = min(tm, 256)

        def rows_of(ref, r0):
            return ref[...] if ref.shape[0] == 1 else ref[pl.ds(r0, chunk), :]

        def body(c, carry):
            r0 = pl.multiple_of(c * chunk, chunk)
            x = x_ref[pl.ds(r0, chunk), :]
            y = x * lax.rsqrt(jnp.mean(x * x, axis=-1, keepdims=True) + EPS)
            h = ((y * g_ref[...]) * (1.0 + rows_of(sc_ref, r0)) + rows_of(sh_ref, r0)).astype(h_ref.dtype)
            h_ref[pl.ds(r0, chunk), :] = h
            return carry

        lax.fori_loop(0, tm // chunk, body, 0)

    o_ref[...] = jnp.dot(h_ref[...], w_ref[...].astype(BF16), preferred_element_type=F32)


def _normed_matmul(x, g, mod, sc_idx, sh_idx, rows, w, tm, tn, name, emit_h):
    m, d = x.shape
    n = w.shape[1]
    out_shape = [jax.ShapeDtypeStruct((m, n), F32)]
    out_specs = [pl.BlockSpec((tm, tn), lambda i, j: (i, j))]
    if emit_h:
        out_shape.append(jax.ShapeDtypeStruct((m, d), BF16))
        out_specs.append(pl.BlockSpec((tm, d), lambda i, j: (i, 0)))
    return pl.pallas_call(
        _normed_mm_kernel,
        grid=(m // tm, n // tn),
        in_specs=[pl.BlockSpec((tm, d), lambda i, j: (i, 0), pipeline_mode=pl.Buffered(1)),
                  pl.BlockSpec((1, d), lambda i, j: (0, 0)),
                  rows.spec(tm, d, sc_idx, follow_j=False),
                  rows.spec(tm, d, sh_idx, follow_j=False),
                  pl.BlockSpec((d, tn), lambda i, j: (0, j))],
        out_specs=out_specs,
        out_shape=out_shape,
        scratch_shapes=[] if emit_h else [pltpu.VMEM((tm, d), BF16)],
        compiler_params=_params("arbitrary", "arbitrary"),
        name=name,
    )(x, g.reshape(1, d), mod, mod, w)


def _merge_kernel(att_ref, sg_ref, ga_ref, gb_ref, wa_ref, wb_ref, o_ref):
    pa = jnp.dot(att_ref[...], wa_ref[...].astype(BF16), preferred_element_type=F32)
    pb = jnp.dot(sg_ref[...], wb_ref[...].astype(BF16), preferred_element_type=F32)
    o_ref[...] = (jax.nn.sigmoid(ga_ref[...]) * pa + jax.nn.sigmoid(gb_ref[...]) * pb).astype(o_ref.dtype)


def _merge(att, sg, proj, ga_col, gb_col, w_pa, w_pb, tm, tn):
    m, ka = att.shape
    kb = sg.shape[1]
    n = w_pa.shape[1]
    ga0, gb0 = ga_col // tn, gb_col // tn
    return pl.pallas_call(
        _merge_kernel,
        grid=(m // tm, n // tn),
        in_specs=[pl.BlockSpec((tm, ka), lambda i, j: (i, 0)),
                  pl.BlockSpec((tm, kb), lambda i, j: (i, 0)),
                  pl.BlockSpec((tm, tn), lambda i, j: (i, j + ga0)),
                  pl.BlockSpec((tm, tn), lambda i, j: (i, j + gb0)),
                  pl.BlockSpec((ka, tn), lambda i, j: (0, j)),
                  pl.BlockSpec((kb, tn), lambda i, j: (0, j))],
        out_specs=pl.BlockSpec((tm, tn), lambda i, j: (i, j)),
        out_shape=jax.ShapeDtypeStruct((m, n), BF16),
        compiler_params=_params("parallel", "arbitrary"),
        name="merge",
    )(att, sg, proj, proj, w_pa, w_pb)


def _outproj_kernel(a_ref, w_ref, x_ref, gt_ref, o_ref):
    acc = jnp.dot(a_ref[...], w_ref[...].astype(BF16), preferred_element_type=F32)
    o_ref[...] = x_ref[...] + gt_ref[...] * acc


def _outproj(a, w, x, mod, gt_idx, rows, tm, tn):
    m, k = a.shape
    n = w.shape[1]
    return pl.pallas_call(
        _outproj_kernel,
        grid=(m // tm, n // tn),
        in_specs=[pl.BlockSpec((tm, k), lambda i, j: (i, 0)),
                  pl.BlockSpec((k, tn), lambda i, j: (0, j)),
                  pl.BlockSpec((tm, tn), lambda i, j: (i, j)),
                  rows.spec(tm, tn, gt_idx * (n // tn))],
        out_specs=pl.BlockSpec((tm, tn), lambda i, j: (i, j)),
        out_shape=jax.ShapeDtypeStruct((m, n), F32),
        compiler_params=_params("parallel", "arbitrary"),
        name="outproj",
    )(a, w, x, mod)


def _softmax_parts(s):
    m = jnp.max(s, axis=-1, keepdims=True)
    p = jnp.exp(s - m)
    return m, p, jnp.sum(p, axis=-1, keepdims=True)


def _combine_groups(outs, lses):
    top = functools.reduce(jnp.maximum, lses)
    es = [jnp.exp(l - top) for l in lses]
    tot = functools.reduce(jnp.add, es)
    return functools.reduce(jnp.add, [(e / tot) * o for e, o in zip(es, outs)])


def _attn_prompt_kernel(sl_ref, *refs, seq):
    q_refs, k_refs, v_refs = refs[0:9:3], refs[1:9:3], refs[2:9:3]
    o_ref, og_ref, lg_ref = refs[9], refs[10], refs[11]
    h = pl.program_id(1)
    scale = HEAD_DIM ** -0.5
    span = ATT_SPAN
    back1 = lax.broadcasted_iota(jnp.int32, (span, span), 0) - lax.broadcasted_iota(jnp.int32, (span, span), 1)
    back2 = (lax.broadcasted_iota(jnp.int32, (span, 2 * span), 0) + span
             - lax.broadcasted_iota(jnp.int32, (span, 2 * span), 1))
    for g in range(N_GROUPS):
        dil = DILATIONS[g]
        n_blocks = seq // dil // span
        step_bias = sl_ref[g, h] * float(dil)
        bias1 = jnp.where(back1 >= 0, -(step_bias * back1.astype(F32)), -jnp.inf)
        bias2 = jnp.where((back2 >= 0) & (back2 <= span), -(step_bias * back2.astype(F32)), -jnp.inf)
        blocks = [(res, blk) for blk in range(n_blocks) for res in range(dil)]
        n_first = dil
        rows = [pl.ds(res + dil * span * blk, span) if dil == 1 else pl.ds(res + dil * span * blk, span, stride=dil)
                for res, blk in blocks]
        q = jnp.stack([q_refs[g][r, :] for r in rows]).astype(BF16)
        k_cur = [k_refs[g][r, :].astype(BF16) for r in rows]
        v_cur = [v_refs[g][r, :].astype(BF16) for r in rows]
        if n_blocks == 1:
            k, v = jnp.stack(k_cur), jnp.stack(v_cur)
            bias = bias1[None]
        else:
            prev = [i if blk == 0 else blocks.index((res, blk - 1)) for i, (res, blk) in enumerate(blocks)]
            k = jnp.stack([jnp.concatenate([k_cur[j], k_cur[i]], axis=0) for i, j in enumerate(prev)])
            v = jnp.stack([jnp.concatenate([v_cur[j], v_cur[i]], axis=0) for i, j in enumerate(prev)])
            bias_first = jnp.concatenate([jnp.full((span, span), -jnp.inf, F32), bias1], axis=1)
            bias = jnp.concatenate([jnp.broadcast_to(bias_first[None], (n_first, span, 2 * span)),
                                    jnp.broadcast_to(bias2[None], (len(blocks) - n_first, span, 2 * span))], axis=0)
        s = lax.dot_general(q, k, (((2,), (2,)), ((0,), (0,))), preferred_element_type=F32) * scale + bias
        m, p, den = _softmax_parts(s)
        o = lax.dot_general((p / den).astype(BF16), v, (((2,), (1,)), ((0,), (0,))), preferred_element_type=F32)
        lse = m + jnp.log(den)
        for i, r in enumerate(rows):
            og_ref[g, r, :] = o[i]
            lg_ref[g, r, :] = jnp.broadcast_to(lse[i], (span, HEAD_DIM))

    def combine(i, carry):
        rows = pl.ds(pl.multiple_of(i * span, span), span)
        o_ref[rows, :] = _combine_groups([og_ref[g, rows, :] for g in range(N_GROUPS)],
                                         [lg_ref[g, rows, :] for g in range(N_GROUPS)]).astype(o_ref.dtype)
        return carry

    lax.fori_loop(0, seq // span, combine, 0)


def _attn_prompt(qkv, slopes, batch, seq):
    ncols = qkv.shape[1]
    assert seq % (max(DILATIONS) * ATT_SPAN) == 0, "every residue class needs whole query blocks"
    qkv3 = qkv.reshape(batch, seq, ncols)
    in_specs = [pl.BlockSpec(memory_space=pltpu.SMEM)]
    for g in range(N_GROUPS):
        for part in range(3):
            cb = part * N_HEADS_ATT + g * HEADS_PER_GROUP
            in_specs.append(pl.BlockSpec((None, seq, HEAD_DIM), lambda b, h, cb=cb: (b, 0, cb + h)))
    out = pl.pallas_call(
        functools.partial(_attn_prompt_kernel, seq=seq),
        grid=(batch, HEADS_PER_GROUP),
        in_specs=in_specs,
        out_specs=pl.BlockSpec((None, seq, HEAD_DIM), lambda b, h: (b, 0, h)),
        out_shape=jax.ShapeDtypeStruct((batch, seq, HEADS_PER_GROUP * HEAD_DIM), BF16),
        scratch_shapes=[pltpu.VMEM((N_GROUPS, seq, HEAD_DIM), F32),
                        pltpu.VMEM((N_GROUPS, seq, HEAD_DIM), F32)],
        compiler_params=_params("parallel", "arbitrary"),
        name="attn_prompt",
    )(slopes, *([qkv3] * 9))
    return out.reshape(batch * seq, HEADS_PER_GROUP * HEAD_DIM)


def _attn_sample_kernel(sl_ref, qkv_ref, k0, v0, k1, v1, k2, v2, o_ref, *, dec_seq):
    scale = HEAD_DIM ** -0.5
    att_w = N_HEADS_ATT * HEAD_DIM
    nh = HEADS_PER_GROUP
    caches = ((k0, v0), (k1, v1), (k2, v2))
    batch_nt = (((2,), (2,)), ((0,), (0,)))
    batch_nn = (((2,), (1,)), ((0,), (0,)))
    t_new = lax.broadcasted_iota(jnp.int32, (dec_seq, dec_seq), 0)
    c_new = lax.broadcasted_iota(jnp.int32, (dec_seq, dec_seq), 1)
    outs, lses = [], []
    for g in range(N_GROUPS):
        dil, win = DILATIONS[g], WINDOWS[g]

        def new_rows(part):
            cols = [part * att_w + (g * nh + h) * HEAD_DIM for h in range(nh)]
            return jnp.stack([qkv_ref[:, c:c + HEAD_DIM] for c in cols]).astype(BF16)

        def cached(ref):
            per_head = []
            for h in range(nh):
                head_rows = pl.ds(h, ref.shape[-2] // nh, stride=nh)
                per_head.append(ref[:, head_rows, :].reshape(-1, HEAD_DIM) if g == 2 else ref[head_rows, :])
            return jnp.stack(per_head).astype(BF16)

        q, kn, vn = new_rows(0), new_rows(1), new_rows(2)
        kc, vc = cached(caches[g][0]), cached(caches[g][1])
        rows = kc.shape[1]
        t_c = lax.broadcasted_iota(jnp.int32, (dec_seq, rows), 0)
        c_c = lax.broadcasted_iota(jnp.int32, (dec_seq, rows), 1)
        e_c = ((c_c >> 3) << 4) + (c_c & 7) if g == 2 else c_c
        d_c = win + t_c - e_c
        ok_c = ((d_c & (dil - 1)) == 0) & (d_c <= ATT_SPAN * dil)
        d_n = t_new - c_new
        ok_n = (d_n >= 0) & ((d_n & (dil - 1)) == 0)
        slope = jnp.stack([jnp.full((1, 1), sl_ref[g, h], F32) for h in range(nh)])
        s_c = lax.dot_general(q, kc, batch_nt, preferred_element_type=F32) * scale
        s_n = lax.dot_general(q, kn, batch_nt, preferred_element_type=F32) * scale
        s_c = jnp.where(ok_c[None], s_c - slope * d_c.astype(F32)[None], -jnp.inf)
        s_n = jnp.where(ok_n[None], s_n - slope * d_n.astype(F32)[None], -jnp.inf)
        m = jnp.maximum(jnp.max(s_c, axis=-1, keepdims=True), jnp.max(s_n, axis=-1, keepdims=True))
        p_c = jnp.exp(s_c - m)
        p_n = jnp.exp(s_n - m)
        den = jnp.sum(p_c, axis=-1, keepdims=True) + jnp.sum(p_n, axis=-1, keepdims=True)
        outs.append(lax.dot_general((p_c / den).astype(BF16), vc, batch_nn, preferred_element_type=F32)
                    + lax.dot_general((p_n / den).astype(BF16), vn, batch_nn, preferred_element_type=F32))
        lses.append(m + jnp.log(den))
    combined = _combine_groups(outs, lses)
    for h in range(nh):
        o_ref[:, h * HEAD_DIM:(h + 1) * HEAD_DIM] = combined[h]


def _attn_sample(qkv, slopes, caches, dec_batch, dec_seq):
    att_w = N_HEADS_ATT * HEAD_DIM
    hw = HEADS_PER_GROUP * HEAD_DIM
    ins, specs = [], []
    for g in range(N_GROUPS):
        for c in caches[g]:
            win = c.shape[2]
            if g == 2:
                dil = DILATIONS[g]
                assert dec_seq <= dil and dec_seq % 8 == 0
                ins.append(c.reshape(1, dec_batch, win // dil, dil * HEADS_PER_GROUP, HEAD_DIM))
                specs.append(pl.BlockSpec((None, None, win // dil, dec_seq * HEADS_PER_GROUP, HEAD_DIM),
                                          lambda b: (0, b, 0, 0, 0)))
            else:
                ins.append(c.reshape(1, dec_batch, win * HEADS_PER_GROUP, HEAD_DIM))
                specs.append(pl.BlockSpec((None, None, win * HEADS_PER_GROUP, HEAD_DIM), lambda b: (0, b, 0, 0)))
    return pl.pallas_call(
        functools.partial(_attn_sample_kernel, dec_seq=dec_seq),
        grid=(dec_batch,),
        in_specs=[pl.BlockSpec(memory_space=pltpu.SMEM),
                  pl.BlockSpec((dec_seq, 3 * att_w), lambda b: (b, 0))] + specs,
        out_specs=pl.BlockSpec((dec_seq, hw), lambda b: (b, 0)),
        out_shape=jax.ShapeDtypeStruct((dec_batch * dec_seq, hw), F32),
        compiler_params=_params("parallel"),
        name="attn_sample",
    )(slopes, qkv, *ins)


def _sgu_kernel(*refs, block_rows, parts, want_vn):
    su_refs, sv_refs = refs[:parts], refs[parts:2 * parts]
    lg_ref, lb_ref, w_ref, b_ref, sg_ref = refs[2 * parts:2 * parts + 5]

    def whole(rs):
        return rs[0][...] if parts == 1 else jnp.concatenate([r[...] for r in rs], axis=1)

    u = _gelu(whole(su_refs))
    v = _gelu(whole(sv_refs))
    mu = jnp.mean(v, axis=-1, keepdims=True)
    vc = v - mu
    vn = vc * lax.rsqrt(jnp.mean(vc * vc, axis=-1, keepdims=True) + EPS) * lg_ref[...] + lb_ref[...]
    if want_vn:
        refs[2 * parts + 5][...] = vn
    r = lax.broadcasted_iota(jnp.int32, (CHUNK, CHUNK), 0)
    c = lax.broadcasted_iota(jnp.int32, (CHUNK, CHUNK), 1)
    keep = (c <= r) & ((r // block_rows) == (c // block_rows))
    ch = vn.shape[1] // SG_GROUPS
    for g in range(SG_GROUPS):
        w = jnp.where(keep, w_ref[g], 0.0).astype(BF16)
        mixed = jnp.dot(w, vn[:, g * ch:(g + 1) * ch].astype(BF16), preferred_element_type=F32) + b_ref[:, g:g + 1]
        sg_ref[:, g * ch:(g + 1) * ch] = (u[:, g * ch:(g + 1) * ch] * mixed).astype(sg_ref.dtype)


def _sgu(proj, su_col, sv_col, d_sg, ln_g, ln_b, w, bias_t, block_rows, want_vn):
    m = proj.shape[0]
    out_shape = [jax.ShapeDtypeStruct((m, d_sg), BF16)]
    out_specs = [pl.BlockSpec((CHUNK, d_sg), lambda i: (i, 0))]
    if want_vn:
        out_shape.append(jax.ShapeDtypeStruct((m, d_sg), F32))
        out_specs.append(pl.BlockSpec((CHUNK, d_sg), lambda i: (i, 0)))
    unit = math.gcd(math.gcd(su_col, sv_col), d_sg)
    parts = d_sg // unit
    col_specs = [pl.BlockSpec((CHUNK, unit), lambda i, cb=c0 // unit + k: (i, cb))
                 for c0 in (su_col, sv_col) for k in range(parts)]
    return pl.pallas_call(
        functools.partial(_sgu_kernel, block_rows=block_rows, parts=parts, want_vn=want_vn),
        grid=(m // CHUNK,),
        in_specs=col_specs + [pl.BlockSpec((1, d_sg), lambda i: (0, 0)),
                              pl.BlockSpec((1, d_sg), lambda i: (0, 0)),
                              pl.BlockSpec((SG_GROUPS, CHUNK, CHUNK), lambda i: (0, 0, 0)),
                              pl.BlockSpec((CHUNK, SG_GROUPS), lambda i: (0, 0))],
        out_specs=out_specs,
        out_shape=out_shape,
        compiler_params=_params("parallel"),
        name="sgu",
    )(*([proj] * (2 * parts)), ln_g.reshape(1, d_sg), ln_b.reshape(1, d_sg), w, bias_t)


def _take_top(s, key_id, n_take):
    rows, cols = s.shape
    slot = lax.broadcasted_iota(jnp.int32, (n_take, cols), 0)
    rank = jnp.full((rows, cols), float(n_take), F32)
    top = jnp.zeros((n_take, cols), F32)
    for r in range(n_take):
        m = jnp.max(s, axis=0, keepdims=True)
        first = jnp.min(jnp.where(s == m, key_id, float(rows)), axis=0, keepdims=True)
        sel = key_id == first
        rank = jnp.where(sel, float(r), rank)
        s = jnp.where(sel, -jnp.inf, s)
        top = jnp.where(slot == r, m, top)
    return top, rank


MARK = 2.0 ** 100
MARK_STEP = 2.0 ** 93
MARK_FLOOR = -(2.0 ** 90)


def _take_top_unique(s, n_take):
    cols = s.shape[1]
    slot = lax.broadcasted_iota(jnp.int32, (n_take, cols), 0)
    top = jnp.zeros((n_take, cols), F32)
    for r in range(n_take):
        m = jnp.max(s, axis=0, keepdims=True)
        top = jnp.where(slot == r, m, top)
        s = jnp.where(s == m, -(MARK + r * MARK_STEP), s)
    removed = s <= -MARK
    rank = jnp.where(removed, (-s - MARK) * (1.0 / MARK_STEP), float(n_take))
    return top, rank, jnp.sum(jnp.where(removed, 1.0, 0.0), axis=0, keepdims=True)


def _candidate_pieces(top0, top1):
    tt = top0.shape[1]
    sub = lax.broadcasted_iota(jnp.int32, (8, tt), 0)
    subf = sub.astype(F32)
    pieces, flat = [], []
    for a in range(8):
        n_b = PEER_TOPK // (a + 1)
        for b0 in range(0, n_b, 8):
            c = top0[a:a + 1, :] + top1[b0:b0 + 8, :]
            pieces.append(jnp.where(sub < n_b - b0, c, -jnp.inf))
            flat.append(subf + float(a * PEER_TOPK + b0))
    pieces.append(top0[8:16, :] + top1[0:1, :])
    flat.append((subf + 8.0) * float(PEER_TOPK))
    return pieces, flat


def _write_route(scores, top0, top1, rank0, rank1, pieces, taken, rk_ref, e1_ref, lk_ref, e0_ref):
    nk, tt = rank0.shape
    best = top0[0:1, :] + top1[0:1, :]
    z = functools.reduce(jnp.add, [jnp.sum(jnp.where(t > 0.0, jnp.exp(c - best), 0.0), axis=0, keepdims=True)
                                   for t, c in zip(taken, pieces)])
    counts = [jnp.sum(taken[0] + taken[1], axis=0, keepdims=True)]
    counts += [jnp.sum(taken[a + 1], axis=0, keepdims=True) for a in range(1, 8)]
    counts += [taken[-1][a:a + 1, :] for a in range(8)]
    lk = jnp.zeros((nk, tt), F32)
    for a in range(PEER_TOPK):
        lk = jnp.where(rank0 == float(a), counts[a], lk)
    rk_ref[...] = rank1.astype(rk_ref.dtype)
    e1_ref[...] = jnp.exp(scores[1] - top1[0:1, :]).astype(e1_ref.dtype)
    lk_ref[...] = lk
    e0_ref[...] = jnp.exp(scores[0] - top0[0:1, :]) / z


def _peer_route_kernel(q_ref, sk_ref, *refs):
    tt = q_ref.shape[0]
    half = q_ref.shape[1] // 2
    nk = sk_ref.shape[1]
    if len(refs) > 4:
        t0_ref, t1_ref, c0_ref, c1_ref = refs[0], refs[1], refs[6], refs[7]
        c0_ref[...] = t0_ref[...].astype(BF16)
        c1_ref[...] = t1_ref[...].astype(BF16)
        refs = refs[2:6]
    out_refs = refs
    scores = [lax.dot_general(sk_ref[p].astype(BF16), q_ref[:, p * half:(p + 1) * half].astype(BF16),
                              NT_DIMS, preferred_element_type=F32) for p in range(2)]

    top0, rank0, removed0 = _take_top_unique(scores[0], PEER_TOPK)
    top1, rank1, removed1 = _take_top_unique(scores[1], PEER_TOPK)
    pieces, _ = _candidate_pieces(top0, top1)
    cand = list(pieces)
    for _ in range(PEER_TOPK):
        m = jnp.max(functools.reduce(jnp.maximum, cand), axis=0, keepdims=True)
        cand = [jnp.where(c == m, -MARK, c) for c in cand]
    taken = [jnp.where(c == -MARK, 1.0, 0.0) for c in cand]
    n_taken = functools.reduce(jnp.add, [jnp.sum(t, axis=0, keepdims=True) for t in taken])
    lowest = jnp.minimum(jnp.min(scores[0], axis=0, keepdims=True), jnp.min(scores[1], axis=0, keepdims=True))
    k = float(PEER_TOPK)
    unsure = (removed0 != k) | (removed1 != k) | (n_taken != k) | (lowest < MARK_FLOOR)
    _write_route(scores, top0, top1, rank0, rank1, pieces, taken, *out_refs)

    @pl.when(jnp.sum(jnp.where(unsure, 1.0, 0.0)) > 0.0)
    def _():
        key_id = lax.broadcasted_iota(jnp.int32, (nk, tt), 0).astype(F32)
        xtop0, xrank0 = _take_top(scores[0], key_id, PEER_TOPK)
        xtop1, xrank1 = _take_top(scores[1], key_id, PEER_TOPK)
        xpieces, flat = _candidate_pieces(xtop0, xtop1)
        xcand = list(xpieces)
        xtaken = [jnp.zeros((8, tt), F32) for _ in xpieces]
        big = float(PEER_TOPK * PEER_TOPK)
        for _ in range(PEER_TOPK):
            m = jnp.max(functools.reduce(jnp.maximum, xcand), axis=0, keepdims=True)
            first = jnp.min(functools.reduce(jnp.minimum,
                                             [jnp.where(c == m, f, big) for c, f in zip(xcand, flat)]),
                            axis=0, keepdims=True)
            for i, f in enumerate(flat):
                sel = f == first
                xcand[i] = jnp.where(sel, -jnp.inf, xcand[i])
                xtaken[i] = jnp.where(sel, 1.0, xtaken[i])
        _write_route(scores, xtop0, xtop1, xrank0, xrank1, xpieces, xtaken, *out_refs)


def _peer_route(qp, sub_keys, tt, tables=()):
    m = qp.shape[0]
    nh, _, nk, half = sub_keys.shape
    ospec = pl.BlockSpec((None, nk, tt), lambda i, h: (h, 0, i))
    in_specs = [pl.BlockSpec((tt, 2 * half), lambda i, h: (i, h)),
                pl.BlockSpec((None, 2, nk, half), lambda i, h: (h, 0, 0, 0))]
    out_specs = [ospec] * 4
    out_shape = [jax.ShapeDtypeStruct((nh, nk, m), dt) for dt in (GATE_DTYPE, GATE_DTYPE, F32, F32)]
    if tables:
        rows, width = tables[0].shape
        n_steps = (m // tt) * nh
        rb = rows // n_steps
        assert rows % n_steps == 0 and rb % 16 == 0, (rows, n_steps)
        tspec = pl.BlockSpec((rb, width), lambda i, h: (i * nh + h, 0))
        in_specs += [tspec, tspec]
        out_specs += [tspec, tspec]
        out_shape += [jax.ShapeDtypeStruct((rows, width), BF16)] * 2
    return pl.pallas_call(
        _peer_route_kernel,
        grid=(m // tt, nh),
        in_specs=in_specs,
        out_specs=out_specs,
        out_shape=out_shape,
        compiler_params=_params("arbitrary", "arbitrary"),
        name="peer_route",
    )(qp, sub_keys, *tables)


MXU_COLS = 256


def _peer_dense_kernel(h_ref, u_ref, v_ref, rk_ref, e1_ref, lk_ref, e0_ref, o_ref, act_ref, g_ref, *,
                       n_chunks, n_items):
    s = pl.program_id(0)
    _, tt, ec = act_ref.shape
    nk = rk_ref.shape[1]
    d = v_ref.shape[1]
    first_key = (jnp.clip(s - 1, 0, n_items - 1) % n_chunks) * (ec // nk)
    down_chunk = jnp.clip(s - 2, 0, n_items - 1) % n_chunks

    @pl.when(s == 0)
    def _():
        act_ref[...] = jnp.zeros(act_ref.shape, act_ref.dtype)
        g_ref[...] = jnp.zeros(g_ref.shape, g_ref.dtype)

    @pl.when(down_chunk == 0)
    def _():
        o_ref[...] = jnp.zeros(o_ref.shape, o_ref.dtype)

    n_tiles = d // MXU_COLS
    pieces = [(ii, tb) for ii in range(ec // nk) for tb in range(tt // nk)]
    per_tile = -(-len(pieces) // n_tiles)
    half = ec // 2

    def step(rs, ws):
        key_rows = {}

        def key_row(ref, hd, ii):
            if (id(ref), hd, ii) not in key_rows:
                key_rows[id(ref), hd, ii] = ref[hd, pl.ds(first_key + ii, 1), :].astype(GATE_DTYPE)
            return key_rows[id(ref), hd, ii]

        for n in range(n_tiles):
            if n == 0:
                act_ref[ws, :, :half] = lax.dot_general(h_ref[...], u_ref[:half, :], NT_DIMS,
                                                        preferred_element_type=F32)
            if n == n_tiles // 2:
                act_ref[ws, :, half:] = lax.dot_general(h_ref[...], u_ref[half:, :], NT_DIMS,
                                                        preferred_element_type=F32)
            cols = slice(n * MXU_COLS, (n + 1) * MXU_COLS)
            o_ref[:, cols] += jnp.dot(g_ref[rs], v_ref[:, cols], preferred_element_type=F32)
            for ii, tb in pieces[n * per_tile:(n + 1) * per_tile]:
                toks = slice(tb * nk, (tb + 1) * nk)
                exps = slice(ii * nk, (ii + 1) * nk)
                w = None
                for hd in range(rk_ref.shape[0]):
                    take = rk_ref[hd, :, toks] < key_row(lk_ref, hd, ii)[:, toks]
                    term = (jnp.where(take, e1_ref[hd, :, toks], jnp.zeros((), GATE_DTYPE))
                            * key_row(e0_ref, hd, ii)[:, toks])
                    w = term if w is None else w + term
                g_ref[ws, toks, exps] = (w.astype(F32).T * _gelu(act_ref[rs, toks, exps])).astype(BF16)

    @pl.when(s % 2 == 0)
    def _():
        step(1, 0)

    @pl.when(s % 2 == 1)
    def _():
        step(0, 1)


def _peer_dense(h2, u_bf, v_bf, route, tt, ec):
    m, d = h2.shape
    n_chunks = u_bf.shape[0] // ec
    n_items = (m // tt) * n_chunks
    nh, nk, _ = route[0].shape

    def up_item(s):
        return jnp.minimum(s, n_items - 1)

    def gate_item(s):
        return jnp.clip(s - 1, 0, n_items - 1)

    def down_item(s):
        return jnp.clip(s - 2, 0, n_items - 1)

    rspec = pl.BlockSpec((nh, nk, tt), lambda s: (0, 0, gate_item(s) // n_chunks), pipeline_mode=pl.Buffered(1))
    return pl.pallas_call(
        functools.partial(_peer_dense_kernel, n_chunks=n_chunks, n_items=n_items),
        grid=(n_items + 2,),
        in_specs=[pl.BlockSpec((tt, d), lambda s: (up_item(s) // n_chunks, 0), pipeline_mode=pl.Buffered(1)),
                  pl.BlockSpec((ec, d), lambda s: (up_item(s) % n_chunks, 0)),
                  pl.BlockSpec((ec, d), lambda s: (down_item(s) % n_chunks, 0)),
                  rspec, rspec, rspec, rspec],
        out_specs=pl.BlockSpec((tt, d), lambda s: (down_item(s) // n_chunks, 0)),
        out_shape=jax.ShapeDtypeStruct((m, d), F32),
        scratch_shapes=[pltpu.VMEM((2, tt, ec), F32), pltpu.VMEM((2, tt, ec), BF16)],
        compiler_params=_params("arbitrary"),
        name="peer_dense",
    )(h2, u_bf, v_bf, *route)


def _final_kernel(x_ref, p_ref, gt_ref, g_ref, o_ref):
    x = x_ref[...] + gt_ref[...] * p_ref[...]
    o_ref[...] = x * lax.rsqrt(jnp.mean(x * x, axis=-1, keepdims=True) + EPS) * g_ref[...]


def _final(x, peer, mod, gt_idx, g, rows, tm):
    m, d = x.shape
    return pl.pallas_call(
        _final_kernel,
        grid=(m // tm, 1),
        in_specs=[pl.BlockSpec((tm, d), lambda i, j: (i, 0)),
                  pl.BlockSpec((tm, d), lambda i, j: (i, 0)),
                  rows.spec(tm, d, gt_idx),
                  pl.BlockSpec((1, d), lambda i, j: (0, 0))],
        out_specs=pl.BlockSpec((tm, d), lambda i, j: (i, 0)),
        out_shape=jax.ShapeDtypeStruct((m, d), F32),
        compiler_params=_params("parallel", "arbitrary"),
        name="final",
    )(x, peer, mod, g.reshape(1, d))


def _tile(n, want):
    t = min(n, want)
    assert n % t == 0, (n, want)
    return t


def _layer(x, mod, rows, attend, sgu_block_rows, want_vn, p):
    m, d = x.shape
    d_sg = d // 2
    att_w = N_HEADS_ATT * HEAD_DIM
    tm = _tile(m, 1024)
    tm_row = _tile(m, 256)
    proj, = _normed_matmul(x, p["norm1_g"], mod, 1, 0, rows, p["w_in"], tm, 512, "inproj", False)
    att = attend(proj)
    sg_out = _sgu(proj, 3 * att_w, 3 * att_w + d_sg, d_sg, p["sg_ln_g"], p["sg_ln_b"],
                  p["sg_w_eff"][sgu_block_rows], p["sg_b_eff"][sgu_block_rows], sgu_block_rows, want_vn)
    sg = sg_out[0]
    merged = _merge(att.astype(BF16), sg, proj, 3 * att_w + 2 * d_sg, 3 * att_w + 2 * d_sg + d,
                    p["w_pa"], p["w_pb"], tm, 512)
    x1 = _outproj(merged, p["w_o"], x, mod, 2, rows, tm, 512)
    qp, h2 = _normed_matmul(x1, p["norm2_g"], mod, 4, 3, rows, p["peer_wq"], _tile(m, 512), 512, "peer_q", True)
    tt = _tile(m, 512)
    if "peer_u_bf" in p:
        route = _peer_route(qp, p["peer_subkeys"], tt)
    else:
        *route, p["peer_u_bf"], p["peer_v_bf"] = _peer_route(qp, p["peer_subkeys"], tt, (p["peer_u"], p["peer_v"]))
    peer = _peer_dense(h2, p["peer_u_bf"], p["peer_v_bf"], route, tt, _tile(p["peer_u_bf"].shape[0], 512))
    y = _final(x1, peer, mod, 5, p["final_g"], rows, tm_row)
    return y, proj, (sg_out[1] if want_vn else None)


def kernel(x_prompt, x_sample, cache_k0, cache_v0, cache_k1, cache_v1, cache_k2, cache_v2, c_prompt, c_sample,
           ada_w, ada_b, norm1_g, norm2_g, w_in, sg_ln_g, sg_ln_b, sg_w, sg_b, w_pa, w_pb, w_o, peer_wq,
           peer_subkeys, peer_u, peer_v, final_g):
    batch, seq, d = x_prompt.shape
    dec_batch, dec_seq, _ = x_sample.shape
    depth = ada_w.shape[0]
    assert depth == 1, "single-layer step"
    att_w = N_HEADS_ATT * HEAD_DIM
    hw = HEADS_PER_GROUP * HEAD_DIM

    slopes = jnp.exp2(-8.0 * jnp.arange(1, N_HEADS_ATT + 1, dtype=F32) / N_HEADS_ATT).reshape(N_GROUPS, HEADS_PER_GROUP)
    sg_wl, sg_bl = sg_w[0], sg_b[0]
    reps = CHUNK // dec_seq
    p = dict(norm1_g=norm1_g[0], norm2_g=norm2_g[0], w_in=w_in[0], sg_ln_g=sg_ln_g[0], sg_ln_b=sg_ln_b[0],
             w_pa=w_pa[0], w_pb=w_pb[0], w_o=w_o[0], peer_wq=peer_wq[0], peer_subkeys=peer_subkeys[0],
             peer_u=peer_u[0], peer_v=peer_v[0], final_g=final_g,
             sg_w_eff={CHUNK: sg_wl, dec_seq: jnp.tile(sg_wl[:, :dec_seq, :dec_seq], (1, reps, reps))},
             sg_b_eff={CHUNK: sg_bl.T, dec_seq: jnp.tile(sg_bl.T[:dec_seq], (reps, 1))})

    c_all = jnp.concatenate([c_prompt, c_sample], axis=0)
    mod = _modulation(c_all, ada_w[0], ada_b[0])
    mod_p = mod[:batch].reshape(batch, 1, N_MOD * d)
    mod_s = jnp.repeat(mod[batch:], dec_seq, axis=0)

    y_p, proj_p, _ = _layer(
        x_prompt.reshape(batch * seq, d), mod_p, _Rows(True, seq),
        lambda proj: _attn_prompt(proj, slopes, batch, seq), CHUNK, False, p)

    caches = [(cache_k0, cache_v0), (cache_k1, cache_v1), (cache_k2, cache_v2)]
    y_s, proj_s, vn_s = _layer(
        x_sample.reshape(dec_batch * dec_seq, d), mod_s, _Rows(False, dec_seq),
        lambda proj: _attn_sample(proj, slopes, caches, dec_batch, dec_seq), dec_seq, True, p)

    outs = [y_p.reshape(batch, seq, d), y_s.reshape(dec_batch, dec_seq, d)]
    kp = proj_p.reshape(batch, seq, -1)
    ks = proj_s.reshape(dec_batch, dec_seq, -1)
    for g in range(N_GROUPS):
        keep = min(WINDOWS[g], seq)
        for src, rows_from in ((kp, seq - keep), (ks, 0)):
            for part in (1, 2):
                c0 = part * att_w + g * hw
                blk = src[:, rows_from:, c0:c0 + hw]
                outs.append(blk.reshape(1, blk.shape[0], blk.shape[1], HEADS_PER_GROUP, HEAD_DIM))
    outs.append(vn_s.reshape(1, dec_batch, dec_seq, d // 2))
    return tuple(outs)
```

```python
import functools
import math

import jax
import jax.numpy as jnp
from jax import lax
from jax.experimental import pallas as pl
from jax.experimental.pallas import tpu as pltpu

F32 = jnp.float32
BF16 = jnp.bfloat16
EPS = 1e-6
GATE_DTYPE = jnp.float32

VMEM_LIMIT_BYTES = 58 * 1024 * 1024
LANES = 128

HEAD_DIM = 128
N_GROUPS = 3
HEADS_PER_GROUP = 8
N_HEADS_ATT = N_GROUPS * HEADS_PER_GROUP
WINDOWS = (128, 512, 2048)
DILATIONS = (1, 4, 16)
ATT_SPAN = 128
CHUNK = 128
SG_GROUPS = 8
PEER_HEADS = 8
PEER_NKEYS = 128
PEER_TOPK = 16
N_MOD = 6

NT_DIMS = (((1,), (1,)), ((), ()))


def _params(*sem):
    return pltpu.CompilerParams(dimension_semantics=sem, vmem_limit_bytes=VMEM_LIMIT_BYTES)


def _gelu(x):
    return x * (lax.erf(x * (1.0 / math.sqrt(2.0))) + 1.0) * 0.5


class _Rows:
    def __init__(self, shared, rows_per_batch):
        self.shared = shared
        self.rows_per_batch = rows_per_batch

    def spec(self, tm, tn, col_block0=0):
        if self.shared:
            tpb = self.rows_per_batch // tm
            return pl.BlockSpec((None, 1, tn), lambda i, j: (i // tpb, 0, j + col_block0))
        return pl.BlockSpec((tm, tn), lambda i, j: (i, j + col_block0))


def _mod_kernel(c_ref, w_ref, b_ref, o_ref):
    c = c_ref[...]
    a = (c * jax.nn.sigmoid(c)).astype(BF16)
    o_ref[...] = jnp.dot(a, w_ref[...].astype(BF16), preferred_element_type=F32) + b_ref[...]


def _modulation(c, w, b, tn=1024):
    n, d = c.shape
    cols = w.shape[1]
    return pl.pallas_call(
        _mod_kernel,
        grid=(cols // tn,),
        in_specs=[pl.BlockSpec((n, d), lambda j: (0, 0)),
                  pl.BlockSpec((d, tn), lambda j: (0, j)),
                  pl.BlockSpec((1, tn), lambda j: (0, j))],
        out_specs=pl.BlockSpec((n, tn), lambda j: (0, j)),
        out_shape=jax.ShapeDtypeStruct((n, cols), F32),
        compiler_params=_params("parallel"),
        name="modulation",
    )(c, w, b.reshape(1, cols))


def _normmod_kernel(x_ref, g_ref, sc_ref, sh_ref, o_ref):
    x = x_ref[...]
    y = x * lax.rsqrt(jnp.mean(x * x, axis=-1, keepdims=True) + EPS)
    o_ref[...] = ((y * g_ref[...]) * (1.0 + sc_ref[...]) + sh_ref[...]).astype(o_ref.dtype)


def _normmod(x, g, mod, sc_idx, sh_idx, rows, tm):
    m, d = x.shape
    return pl.pallas_call(
        _normmod_kernel,
        grid=(m // tm, 1),
        in_specs=[pl.BlockSpec((tm, d), lambda i, j: (i, 0)),
                  pl.BlockSpec((1, d), lambda i, j: (0, 0)),
                  rows.spec(tm, d, sc_idx),
                  rows.spec(tm, d, sh_idx)],
        out_specs=pl.BlockSpec((tm, d), lambda i, j: (i, 0)),
        out_shape=jax.ShapeDtypeStruct((m, d), BF16),
        compiler_params=_params("parallel", "arbitrary"),
        name="normmod",
    )(x, g.reshape(1, d), mod, mod)


def _mm_kernel(a_ref, w_ref, o_ref):
    o_ref[...] = jnp.dot(a_ref[...], w_ref[...].astype(BF16), preferred_element_type=F32).astype(o_ref.dtype)


def _mm_keep_kernel(a_ref, w_ref, o_ref, wb_ref):
    w = w_ref[...].astype(BF16)
    o_ref[...] = jnp.dot(a_ref[...], w, preferred_element_type=F32).astype(o_ref.dtype)

    @pl.when(pl.program_id(0) == 0)
    def _():
        wb_ref[...] = w


def _matmul_keep_weights(a, w, tm, tn, name):
    m, k = a.shape
    n = w.shape[1]
    last = n // tn - 1
    return pl.pallas_call(
        _mm_keep_kernel,
        grid=(m // tm, n // tn),
        in_specs=[pl.BlockSpec((tm, k), lambda i, j: (i, 0)),
                  pl.BlockSpec((k, tn), lambda i, j: (0, j))],
        out_specs=[pl.BlockSpec((tm, tn), lambda i, j: (i, j)),
                   pl.BlockSpec((k, tn), lambda i, j: (0, jnp.where(i == 0, j, last)))],
        out_shape=[jax.ShapeDtypeStruct((m, n), F32), jax.ShapeDtypeStruct((k, n), BF16)],
        compiler_params=_params("arbitrary", "arbitrary"),
        name=name,
    )(a, w)


def _matmul(a, w, tm, tn, name):
    m, k = a.shape
    n = w.shape[1]
    return pl.pallas_call(
        _mm_kernel,
        grid=(m // tm, n // tn),
        in_specs=[pl.BlockSpec((tm, k), lambda i, j: (i, 0)),
                  pl.BlockSpec((k, tn), lambda i, j: (0, j))],
        out_specs=pl.BlockSpec((tm, tn), lambda i, j: (i, j)),
        out_shape=jax.ShapeDtypeStruct((m, n), F32),
        compiler_params=_params("parallel", "arbitrary"),
        name=name,
    )(a, w)


def _merge_kernel(att_ref, sg_ref, ga_ref, gb_ref, wa_ref, wb_ref, o_ref):
    pa = jnp.dot(att_ref[...], wa_ref[...].astype(BF16), preferred_element_type=F32)
    pb = jnp.dot(sg_ref[...], wb_ref[...].astype(BF16), preferred_element_type=F32)
    o_ref[...] = (jax.nn.sigmoid(ga_ref[...]) * pa + jax.nn.sigmoid(gb_ref[...]) * pb).astype(o_ref.dtype)


def _merge(att, sg, proj, ga_col, gb_col, w_pa, w_pb, tm, tn):
    m, ka = att.shape
    kb = sg.shape[1]
    n = w_pa.shape[1]
    ga0, gb0 = ga_col // tn, gb_col // tn
    return pl.pallas_call(
        _merge_kernel,
        grid=(m // tm, n // tn),
        in_specs=[pl.BlockSpec((tm, ka), lambda i, j: (i, 0)),
                  pl.BlockSpec((tm, kb), lambda i, j: (i, 0)),
                  pl.BlockSpec((tm, tn), lambda i, j: (i, j + ga0)),
                  pl.BlockSpec((tm, tn), lambda i, j: (i, j + gb0)),
                  pl.BlockSpec((ka, tn), lambda i, j: (0, j)),
                  pl.BlockSpec((kb, tn), lambda i, j: (0, j))],
        out_specs=pl.BlockSpec((tm, tn), lambda i, j: (i, j)),
        out_shape=jax.ShapeDtypeStruct((m, n), BF16),
        compiler_params=_params("parallel", "arbitrary"),
        name="merge",
    )(att, sg, proj, proj, w_pa, w_pb)


def _outproj_kernel(a_ref, w_ref, x_ref, gt_ref, o_ref):
    acc = jnp.dot(a_ref[...], w_ref[...].astype(BF16), preferred_element_type=F32)
    o_ref[...] = x_ref[...] + gt_ref[...] * acc


def _outproj(a, w, x, mod, gt_idx, rows, tm, tn):
    m, k = a.shape
    n = w.shape[1]
    return pl.pallas_call(
        _outproj_kernel,
        grid=(m // tm, n // tn),
        in_specs=[pl.BlockSpec((tm, k), lambda i, j: (i, 0)),
                  pl.BlockSpec((k, tn), lambda i, j: (0, j)),
                  pl.BlockSpec((tm, tn), lambda i, j: (i, j)),
                  rows.spec(tm, tn, gt_idx * (n // tn))],
        out_specs=pl.BlockSpec((tm, tn), lambda i, j: (i, j)),
        out_shape=jax.ShapeDtypeStruct((m, n), F32),
        compiler_params=_params("parallel", "arbitrary"),
        name="outproj",
    )(a, w, x, mod)


def _softmax_parts(s):
    m = jnp.max(s, axis=-1, keepdims=True)
    p = jnp.exp(s - m)
    return m, p, jnp.sum(p, axis=-1, keepdims=True)


def _combine_groups(outs, lses):
    top = functools.reduce(jnp.maximum, lses)
    es = [jnp.exp(l - top) for l in lses]
    tot = functools.reduce(jnp.add, es)
    return functools.reduce(jnp.add, [(e / tot) * o for e, o in zip(es, outs)])


def _attn_prompt_kernel(sl_ref, *refs, seq):
    q_refs, k_refs, v_refs = refs[0:9:3], refs[1:9:3], refs[2:9:3]
    o_ref, og_ref, lg_ref = refs[9], refs[10], refs[11]
    h = pl.program_id(1)
    scale = HEAD_DIM ** -0.5
    span = ATT_SPAN
    back1 = lax.broadcasted_iota(jnp.int32, (span, span), 0) - lax.broadcasted_iota(jnp.int32, (span, span), 1)
    back2 = (lax.broadcasted_iota(jnp.int32, (span, 2 * span), 0) + span
             - lax.broadcasted_iota(jnp.int32, (span, 2 * span), 1))
    for g in range(N_GROUPS):
        dil = DILATIONS[g]
        n_blocks = seq // dil // span
        step_bias = sl_ref[g, h] * float(dil)
        bias1 = jnp.where(back1 >= 0, -(step_bias * back1.astype(F32)), -jnp.inf)
        bias2 = jnp.where((back2 >= 0) & (back2 <= span), -(step_bias * back2.astype(F32)), -jnp.inf)
        blocks = [(res, blk) for blk in range(n_blocks) for res in range(dil)]
        n_first = dil
        rows = [pl.ds(res + dil * span * blk, span) if dil == 1 else pl.ds(res + dil * span * blk, span, stride=dil)
                for res, blk in blocks]
        q = jnp.stack([q_refs[g][r, :] for r in rows]).astype(BF16)
        k_cur = [k_refs[g][r, :].astype(BF16) for r in rows]
        v_cur = [v_refs[g][r, :].astype(BF16) for r in rows]
        if n_blocks == 1:
            k, v = jnp.stack(k_cur), jnp.stack(v_cur)
            bias = bias1[None]
        else:
            prev = [i if blk == 0 else blocks.index((res, blk - 1)) for i, (res, blk) in enumerate(blocks)]
            k = jnp.stack([jnp.concatenate([k_cur[j], k_cur[i]], axis=0) for i, j in enumerate(prev)])
            v = jnp.stack([jnp.concatenate([v_cur[j], v_cur[i]], axis=0) for i, j in enumerate(prev)])
            bias_first = jnp.concatenate([jnp.full((span, span), -jnp.inf, F32), bias1], axis=1)
            bias = jnp.concatenate([jnp.broadcast_to(bias_first[None], (n_first, span, 2 * span)),
                                    jnp.broadcast_to(bias2[None], (len(blocks) - n_first, span, 2 * span))], axis=0)
        s = lax.dot_general(q, k, (((2,), (2,)), ((0,), (0,))), preferred_element_type=F32) * scale + bias
        m, p, den = _softmax_parts(s)
        o = lax.dot_general((p / den).astype(BF16), v, (((2,), (1,)), ((0,), (0,))), preferred_element_type=F32)
        lse = m + jnp.log(den)
        for i, r in enumerate(rows):
            og_ref[g, r, :] = o[i]
            lg_ref[g, r, :] = jnp.broadcast_to(lse[i], (span, HEAD_DIM))

    def combine(i, carry):
        rows = pl.ds(pl.multiple_of(i * span, span), span)
        o_ref[rows, :] = _combine_groups([og_ref[g, rows, :] for g in range(N_GROUPS)],
                                         [lg_ref[g, rows, :] for g in range(N_GROUPS)]).astype(o_ref.dtype)
        return carry

    lax.fori_loop(0, seq // span, combine, 0)


def _attn_prompt(qkv, slopes, batch, seq):
    ncols = qkv.shape[1]
    assert seq % (max(DILATIONS) * ATT_SPAN) == 0, "every residue class needs whole query blocks"
    qkv3 = qkv.reshape(batch, seq, ncols)
    in_specs = [pl.BlockSpec(memory_space=pltpu.SMEM)]
    for g in range(N_GROUPS):
        for part in range(3):
            cb = part * N_HEADS_ATT + g * HEADS_PER_GROUP
            in_specs.append(pl.BlockSpec((None, seq, HEAD_DIM), lambda b, h, cb=cb: (b, 0, cb + h)))
    out = pl.pallas_call(
        functools.partial(_attn_prompt_kernel, seq=seq),
        grid=(batch, HEADS_PER_GROUP),
        in_specs=in_specs,
        out_specs=pl.BlockSpec((None, seq, HEAD_DIM), lambda b, h: (b, 0, h)),
        out_shape=jax.ShapeDtypeStruct((batch, seq, HEADS_PER_GROUP * HEAD_DIM), BF16),
        scratch_shapes=[pltpu.VMEM((N_GROUPS, seq, HEAD_DIM), F32),
                        pltpu.VMEM((N_GROUPS, seq, HEAD_DIM), F32)],
        compiler_params=_params("parallel", "arbitrary"),
        name="attn_prompt",
    )(slopes, *([qkv3] * 9))
    return out.reshape(batch * seq, HEADS_PER_GROUP * HEAD_DIM)


def _attn_sample_kernel(sl_ref, qkv_ref, k0, v0, k1, v1, k2, v2, o_ref, *, dec_seq):
    scale = HEAD_DIM ** -0.5
    att_w = N_HEADS_ATT * HEAD_DIM
    nh = HEADS_PER_GROUP
    caches = ((k0, v0), (k1, v1), (k2, v2))
    batch_nt = (((2,), (2,)), ((0,), (0,)))
    batch_nn = (((2,), (1,)), ((0,), (0,)))
    t_new = lax.broadcasted_iota(jnp.int32, (dec_seq, dec_seq), 0)
    c_new = lax.broadcasted_iota(jnp.int32, (dec_seq, dec_seq), 1)
    outs, lses = [], []
    for g in range(N_GROUPS):
        dil, win = DILATIONS[g], WINDOWS[g]

        def new_rows(part):
            cols = [part * att_w + (g * nh + h) * HEAD_DIM for h in range(nh)]
            return jnp.stack([qkv_ref[:, c:c + HEAD_DIM] for c in cols]).astype(BF16)

        def cached(ref):
            per_head = []
            for h in range(nh):
                head_rows = pl.ds(h, ref.shape[-2] // nh, stride=nh)
                per_head.append(ref[:, head_rows, :].reshape(-1, HEAD_DIM) if g == 2 else ref[head_rows, :])
            return jnp.stack(per_head).astype(BF16)

        q, kn, vn = new_rows(0), new_rows(1), new_rows(2)
        kc, vc = cached(caches[g][0]), cached(caches[g][1])
        rows = kc.shape[1]
        t_c = lax.broadcasted_iota(jnp.int32, (dec_seq, rows), 0)
        c_c = lax.broadcasted_iota(jnp.int32, (dec_seq, rows), 1)
        e_c = ((c_c >> 3) << 4) + (c_c & 7) if g == 2 else c_c
        d_c = win + t_c - e_c
        ok_c = ((d_c & (dil - 1)) == 0) & (d_c <= ATT_SPAN * dil)
        d_n = t_new - c_new
        ok_n = (d_n >= 0) & ((d_n & (dil - 1)) == 0)
        slope = jnp.stack([jnp.full((1, 1), sl_ref[g, h], F32) for h in range(nh)])
        s_c = lax.dot_general(q, kc, batch_nt, preferred_element_type=F32) * scale
        s_n = lax.dot_general(q, kn, batch_nt, preferred_element_type=F32) * scale
        s_c = jnp.where(ok_c[None], s_c - slope * d_c.astype(F32)[None], -jnp.inf)
        s_n = jnp.where(ok_n[None], s_n - slope * d_n.astype(F32)[None], -jnp.inf)
        m = jnp.maximum(jnp.max(s_c, axis=-1, keepdims=True), jnp.max(s_n, axis=-1, keepdims=True))
        p_c = jnp.exp(s_c - m)
        p_n = jnp.exp(s_n - m)
        den = jnp.sum(p_c, axis=-1, keepdims=True) + jnp.sum(p_n, axis=-1, keepdims=True)
        outs.append(lax.dot_general((p_c / den).astype(BF16), vc, batch_nn, preferred_element_type=F32)
                    + lax.dot_general((p_n / den).astype(BF16), vn, batch_nn, preferred_element_type=F32))
        lses.append(m + jnp.log(den))
    combined = _combine_groups(outs, lses)
    for h in range(nh):
        o_ref[:, h * HEAD_DIM:(h + 1) * HEAD_DIM] = combined[h]


def _attn_sample(qkv, slopes, caches, dec_batch, dec_seq):
    att_w = N_HEADS_ATT * HEAD_DIM
    hw = HEADS_PER_GROUP * HEAD_DIM
    ins, specs = [], []
    for g in range(N_GROUPS):
        for c in caches[g]:
            win = c.shape[2]
            if g == 2:
                dil = DILATIONS[g]
                assert dec_seq <= dil and dec_seq % 8 == 0
                ins.append(c.reshape(1, dec_batch, win // dil, dil * HEADS_PER_GROUP, HEAD_DIM))
                specs.append(pl.BlockSpec((None, None, win // dil, dec_seq * HEADS_PER_GROUP, HEAD_DIM),
                                          lambda b: (0, b, 0, 0, 0)))
            else:
                ins.append(c.reshape(1, dec_batch, win * HEADS_PER_GROUP, HEAD_DIM))
                specs.append(pl.BlockSpec((None, None, win * HEADS_PER_GROUP, HEAD_DIM), lambda b: (0, b, 0, 0)))
    return pl.pallas_call(
        functools.partial(_attn_sample_kernel, dec_seq=dec_seq),
        grid=(dec_batch,),
        in_specs=[pl.BlockSpec(memory_space=pltpu.SMEM),
                  pl.BlockSpec((dec_seq, 3 * att_w), lambda b: (b, 0))] + specs,
        out_specs=pl.BlockSpec((dec_seq, hw), lambda b: (b, 0)),
        out_shape=jax.ShapeDtypeStruct((dec_batch * dec_seq, hw), F32),
        compiler_params=_params("parallel"),
        name="attn_sample",
    )(slopes, qkv, *ins)


def _sgu_kernel(*refs, block_rows, parts, want_vn):
    su_refs, sv_refs = refs[:parts], refs[parts:2 * parts]
    lg_ref, lb_ref, w_ref, b_ref, sg_ref = refs[2 * parts:2 * parts + 5]

    def whole(rs):
        return rs[0][...] if parts == 1 else jnp.concatenate([r[...] for r in rs], axis=1)

    u = _gelu(whole(su_refs))
    v = _gelu(whole(sv_refs))
    mu = jnp.mean(v, axis=-1, keepdims=True)
    vc = v - mu
    vn = vc * lax.rsqrt(jnp.mean(vc * vc, axis=-1, keepdims=True) + EPS) * lg_ref[...] + lb_ref[...]
    if want_vn:
        refs[2 * parts + 5][...] = vn
    r = lax.broadcasted_iota(jnp.int32, (CHUNK, CHUNK), 0)
    c = lax.broadcasted_iota(jnp.int32, (CHUNK, CHUNK), 1)
    keep = (c <= r) & ((r // block_rows) == (c // block_rows))
    ch = vn.shape[1] // SG_GROUPS
    for g in range(SG_GROUPS):
        w = jnp.where(keep, w_ref[g], 0.0).astype(BF16)
        mixed = jnp.dot(w, vn[:, g * ch:(g + 1) * ch].astype(BF16), preferred_element_type=F32) + b_ref[:, g:g + 1]
        sg_ref[:, g * ch:(g + 1) * ch] = (u[:, g * ch:(g + 1) * ch] * mixed).astype(sg_ref.dtype)


def _sgu(proj, su_col, sv_col, d_sg, ln_g, ln_b, w, bias_t, block_rows, want_vn):
    m = proj.shape[0]
    out_shape = [jax.ShapeDtypeStruct((m, d_sg), BF16)]
    out_specs = [pl.BlockSpec((CHUNK, d_sg), lambda i: (i, 0))]
    if want_vn:
        out_shape.append(jax.ShapeDtypeStruct((m, d_sg), F32))
        out_specs.append(pl.BlockSpec((CHUNK, d_sg), lambda i: (i, 0)))
    unit = math.gcd(math.gcd(su_col, sv_col), d_sg)
    parts = d_sg // unit
    col_specs = [pl.BlockSpec((CHUNK, unit), lambda i, cb=c0 // unit + k: (i, cb))
                 for c0 in (su_col, sv_col) for k in range(parts)]
    return pl.pallas_call(
        functools.partial(_sgu_kernel, block_rows=block_rows, parts=parts, want_vn=want_vn),
        grid=(m // CHUNK,),
        in_specs=col_specs + [pl.BlockSpec((1, d_sg), lambda i: (0, 0)),
                              pl.BlockSpec((1, d_sg), lambda i: (0, 0)),
                              pl.BlockSpec((SG_GROUPS, CHUNK, CHUNK), lambda i: (0, 0, 0)),
                              pl.BlockSpec((CHUNK, SG_GROUPS), lambda i: (0, 0))],
        out_specs=out_specs,
        out_shape=out_shape,
        compiler_params=_params("parallel"),
        name="sgu",
    )(*([proj] * (2 * parts)), ln_g.reshape(1, d_sg), ln_b.reshape(1, d_sg), w, bias_t)


def _take_top(s, key_id, n_take):
    rows, cols = s.shape
    slot = lax.broadcasted_iota(jnp.int32, (n_take, cols), 0)
    rank = jnp.full((rows, cols), float(n_take), F32)
    top = jnp.zeros((n_take, cols), F32)
    for r in range(n_take):
        m = jnp.max(s, axis=0, keepdims=True)
        first = jnp.min(jnp.where(s == m, key_id, float(rows)), axis=0, keepdims=True)
        sel = key_id == first
        rank = jnp.where(sel, float(r), rank)
        s = jnp.where(sel, -jnp.inf, s)
        top = jnp.where(slot == r, m, top)
    return top, rank


MARK = 2.0 ** 100
MARK_STEP = 2.0 ** 93
MARK_FLOOR = -(2.0 ** 90)


def _take_top_unique(s, n_take):
    cols = s.shape[1]
    slot = lax.broadcasted_iota(jnp.int32, (n_take, cols), 0)
    top = jnp.zeros((n_take, cols), F32)
    for r in range(n_take):
        m = jnp.max(s, axis=0, keepdims=True)
        top = jnp.where(slot == r, m, top)
        s = jnp.where(s == m, -(MARK + r * MARK_STEP), s)
    removed = s <= -MARK
    rank = jnp.where(removed, (-s - MARK) * (1.0 / MARK_STEP), float(n_take))
    return top, rank, jnp.sum(jnp.where(removed, 1.0, 0.0), axis=0, keepdims=True)


def _candidate_pieces(top0, top1):
    tt = top0.shape[1]
    sub = lax.broadcasted_iota(jnp.int32, (8, tt), 0)
    subf = sub.astype(F32)
    pieces, flat = [], []
    for a in range(8):
        n_b = PEER_TOPK // (a + 1)
        for b0 in range(0, n_b, 8):
            c = top0[a:a + 1, :] + top1[b0:b0 + 8, :]
            pieces.append(jnp.where(sub < n_b - b0, c, -jnp.inf))
            flat.append(subf + float(a * PEER_TOPK + b0))
    pieces.append(top0[8:16, :] + top1[0:1, :])
    flat.append((subf + 8.0) * float(PEER_TOPK))
    return pieces, flat


def _write_route(scores, top0, top1, rank0, rank1, pieces, taken, rk_ref, e1_ref, lk_ref, e0_ref):
    nk, tt = rank0.shape
    best = top0[0:1, :] + top1[0:1, :]
    z = functools.reduce(jnp.add, [jnp.sum(jnp.where(t > 0.0, jnp.exp(c - best), 0.0), axis=0, keepdims=True)
                                   for t, c in zip(taken, pieces)])
    counts = [jnp.sum(taken[0] + taken[1], axis=0, keepdims=True)]
    counts += [jnp.sum(taken[a + 1], axis=0, keepdims=True) for a in range(1, 8)]
    counts += [taken[-1][a:a + 1, :] for a in range(8)]
    lk = jnp.zeros((nk, tt), F32)
    for a in range(PEER_TOPK):
        lk = jnp.where(rank0 == float(a), counts[a], lk)
    rk_ref[...] = rank1.astype(rk_ref.dtype)
    e1_ref[...] = jnp.exp(scores[1] - top1[0:1, :]).astype(e1_ref.dtype)
    lk_ref[...] = lk
    e0_ref[...] = jnp.exp(scores[0] - top0[0:1, :]) / z


def _peer_route_kernel(q_ref, sk_ref, *refs):
    tt = q_ref.shape[0]
    half = q_ref.shape[1] // 2
    nk = sk_ref.shape[1]
    if len(refs) > 4:
        t0_ref, t1_ref, c0_ref, c1_ref = refs[0], refs[1], refs[6], refs[7]
        c0_ref[...] = t0_ref[...].astype(BF16)
        c1_ref[...] = t1_ref[...].astype(BF16)
        refs = refs[2:6]
    out_refs = refs
    scores = [lax.dot_general(sk_ref[p].astype(BF16), q_ref[:, p * half:(p + 1) * half].astype(BF16),
                              NT_DIMS, preferred_element_type=F32) for p in range(2)]

    top0, rank0, removed0 = _take_top_unique(scores[0], PEER_TOPK)
    top1, rank1, removed1 = _take_top_unique(scores[1], PEER_TOPK)
    pieces, _ = _candidate_pieces(top0, top1)
    cand = list(pieces)
    for _ in range(PEER_TOPK):
        m = jnp.max(functools.reduce(jnp.maximum, cand), axis=0, keepdims=True)
        cand = [jnp.where(c == m, -MARK, c) for c in cand]
    taken = [jnp.where(c == -MARK, 1.0, 0.0) for c in cand]
    n_taken = functools.reduce(jnp.add, [jnp.sum(t, axis=0, keepdims=True) for t in taken])
    lowest = jnp.minimum(jnp.min(scores[0], axis=0, keepdims=True), jnp.min(scores[1], axis=0, keepdims=True))
    k = float(PEER_TOPK)
    unsure = (removed0 != k) | (removed1 != k) | (n_taken != k) | (lowest < MARK_FLOOR)
    _write_route(scores, top0, top1, rank0, rank1, pieces, taken, *out_refs)

    @pl.when(jnp.sum(jnp.where(unsure, 1.0, 0.0)) > 0.0)
    def _():
        key_id = lax.broadcasted_iota(jnp.int32, (nk, tt), 0).astype(F32)
        xtop0, xrank0 = _take_top(scores[0], key_id, PEER_TOPK)
        xtop1, xrank1 = _take_top(scores[1], key_id, PEER_TOPK)
        xpieces, flat = _candidate_pieces(xtop0, xtop1)
        xcand = list(xpieces)
        xtaken = [jnp.zeros((8, tt), F32) for _ in xpieces]
        big = float(PEER_TOPK * PEER_TOPK)
        for _ in range(PEER_TOPK):
            m = jnp.max(functools.reduce(jnp.maximum, xcand), axis=0, keepdims=True)
            first = jnp.min(functools.reduce(jnp.minimum,
                                             [jnp.where(c == m, f, big) for c, f in zip(xcand, flat)]),
                            axis=0, keepdims=True)
            for i, f in enumerate(flat):
                sel = f == first
                xcand[i] = jnp.where(sel, -jnp.inf, xcand[i])
                xtaken[i] = jnp.where(sel, 1.0, xtaken[i])
        _write_route(scores, xtop0, xtop1, xrank0, xrank1, xpieces, xtaken, *out_refs)


def _peer_route(qp, sub_keys, tt, tables=()):
    m = qp.shape[0]
    nh, _, nk, half = sub_keys.shape
    ospec = pl.BlockSpec((None, nk, tt), lambda i, h: (h, 0, i))
    in_specs = [pl.BlockSpec((tt, 2 * half), lambda i, h: (i, h)),
                pl.BlockSpec((None, 2, nk, half), lambda i, h: (h, 0, 0, 0))]
    out_specs = [ospec] * 4
    out_shape = [jax.ShapeDtypeStruct((nh, nk, m), dt) for dt in (GATE_DTYPE, GATE_DTYPE, F32, F32)]
    if tables:
        rows, width = tables[0].shape
        n_steps = (m // tt) * nh
        rb = rows // n_steps
        assert rows % n_steps == 0 and rb % 16 == 0, (rows, n_steps)
        tspec = pl.BlockSpec((rb, width), lambda i, h: (i * nh + h, 0))
        in_specs += [tspec, tspec]
        out_specs += [tspec, tspec]
        out_shape += [jax.ShapeDtypeStruct((rows, width), BF16)] * 2
    return pl.pallas_call(
        _peer_route_kernel,
        grid=(m // tt, nh),
        in_specs=in_specs,
        out_specs=out_specs,
        out_shape=out_shape,
        compiler_params=_params("arbitrary", "arbitrary"),
        name="peer_route",
    )(qp, sub_keys, *tables)


MXU_COLS = 256


def _peer_dense_kernel(h_ref, u_ref, v_ref, rk_ref, e1_ref, lk_ref, e0_ref, o_ref, act_ref, g_ref, *,
                       n_chunks, n_items):
    s = pl.program_id(0)
    _, tt, ec = act_ref.shape
    nk = rk_ref.shape[1]
    d = v_ref.shape[1]
    first_key = (jnp.clip(s - 1, 0, n_items - 1) % n_chunks) * (ec // nk)
    down_chunk = jnp.clip(s - 2, 0, n_items - 1) % n_chunks

    @pl.when(s == 0)
    def _():
        act_ref[...] = jnp.zeros(act_ref.shape, act_ref.dtype)
        g_ref[...] = jnp.zeros(g_ref.shape, g_ref.dtype)

    @pl.when(down_chunk == 0)
    def _():
        o_ref[...] = jnp.zeros(o_ref.shape, o_ref.dtype)

    n_tiles = d // MXU_COLS
    pieces = [(ii, tb) for ii in range(ec // nk) for tb in range(tt // nk)]
    per_tile = -(-len(pieces) // n_tiles)
    half = ec // 2

    def step(rs, ws):
        key_rows = {}

        def key_row(ref, hd, ii):
            if (id(ref), hd, ii) not in key_rows:
                key_rows[id(ref), hd, ii] = ref[hd, pl.ds(first_key + ii, 1), :].astype(GATE_DTYPE)
            return key_rows[id(ref), hd, ii]

        for n in range(n_tiles):
            if n == 0:
                act_ref[ws, :, :half] = lax.dot_general(h_ref[...], u_ref[:half, :], NT_DIMS,
                                                        preferred_element_type=F32)
            if n == n_tiles // 2:
                act_ref[ws, :, half:] = lax.dot_general(h_ref[...], u_ref[half:, :], NT_DIMS,
                                                        preferred_element_type=F32)
            cols = slice(n * MXU_COLS, (n + 1) * MXU_COLS)
            o_ref[:, cols] += jnp.dot(g_ref[rs], v_ref[:, cols], preferred_element_type=F32)
            for ii, tb in pieces[n * per_tile:(n + 1) * per_tile]:
                toks = slice(tb * nk, (tb + 1) * nk)
                exps = slice(ii * nk, (ii + 1) * nk)
                w = None
                for hd in range(rk_ref.shape[0]):
                    take = rk_ref[hd, :, toks] < key_row(lk_ref, hd, ii)[:, toks]
                    term = (jnp.where(take, e1_ref[hd, :, toks], jnp.zeros((), GATE_DTYPE))
                            * key_row(e0_ref, hd, ii)[:, toks])
                    w = term if w is None else w + term
                g_ref[ws, toks, exps] = (w.astype(F32).T * _gelu(act_ref[rs, toks, exps])).astype(BF16)

    @pl.when(s % 2 == 0)
    def _():
        step(1, 0)

    @pl.when(s % 2 == 1)
    def _():
        step(0, 1)


def _peer_dense(h2, u_bf, v_bf, route, tt, ec):
    m, d = h2.shape
    n_chunks = u_bf.shape[0] // ec
    n_items = (m // tt) * n_chunks
    nh, nk, _ = route[0].shape

    def up_item(s):
        return jnp.minimum(s, n_items - 1)

    def gate_item(s):
        return jnp.clip(s - 1, 0, n_items - 1)

    def down_item(s):
        return jnp.clip(s - 2, 0, n_items - 1)

    rspec = pl.BlockSpec((nh, nk, tt), lambda s: (0, 0, gate_item(s) // n_chunks), pipeline_mode=pl.Buffered(1))
    return pl.pallas_call(
        functools.partial(_peer_dense_kernel, n_chunks=n_chunks, n_items=n_items),
        grid=(n_items + 2,),
        in_specs=[pl.BlockSpec((tt, d), lambda s: (up_item(s) // n_chunks, 0), pipeline_mode=pl.Buffered(1)),
                  pl.BlockSpec((ec, d), lambda s: (up_item(s) % n_chunks, 0)),
                  pl.BlockSpec((ec, d), lambda s: (down_item(s) % n_chunks, 0)),
                  rspec, rspec, rspec, rspec],
        out_specs=pl.BlockSpec((tt, d), lambda s: (down_item(s) // n_chunks, 0)),
        out_shape=jax.ShapeDtypeStruct((m, d), F32),
        scratch_shapes=[pltpu.VMEM((2, tt, ec), F32), pltpu.VMEM((2, tt, ec), BF16)],
        compiler_params=_params("arbitrary"),
        name="peer_dense",
    )(h2, u_bf, v_bf, *route)


def _final_kernel(x_ref, p_ref, gt_ref, g_ref, o_ref):
    x = x_ref[...] + gt_ref[...] * p_ref[...]
    o_ref[...] = x * lax.rsqrt(jnp.mean(x * x, axis=-1, keepdims=True) + EPS) * g_ref[...]


def _final(x, peer, mod, gt_idx, g, rows, tm):
    m, d = x.shape
    return pl.pallas_call(
        _final_kernel,
        grid=(m // tm, 1),
        in_specs=[pl.BlockSpec((tm, d), lambda i, j: (i, 0)),
                  pl.BlockSpec((tm, d), lambda i, j: (i, 0)),
                  rows.spec(tm, d, gt_idx),
                  pl.BlockSpec((1, d), lambda i, j: (0, 0))],
        out_specs=pl.BlockSpec((tm, d), lambda i, j: (i, 0)),
        out_shape=jax.ShapeDtypeStruct((m, d), F32),
        compiler_params=_params("parallel", "arbitrary"),
        name="final",
    )(x, peer, mod, g.reshape(1, d))


def _tile(n, want):
    t = min(n, want)
    assert n % t == 0, (n, want)
    return t


def _layer(x, mod, rows, attend, sgu_block_rows, want_vn, p):
    m, d = x.shape
    d_sg = d // 2
    att_w = N_HEADS_ATT * HEAD_DIM
    tm = _tile(m, 1024)
    tm_row = _tile(m, 256)
    h1 = _normmod(x, p["norm1_g"], mod, 1, 0, rows, tm_row)
    if "w_in_bf" in p:
        proj = _matmul(h1, p["w_in_bf"], tm, 512, "inproj")
    else:
        proj, p["w_in_bf"] = _matmul_keep_weights(h1, p["w_in"], tm, 512, "inproj")
    att = attend(proj)
    sg_out = _sgu(proj, 3 * att_w, 3 * att_w + d_sg, d_sg, p["sg_ln_g"], p["sg_ln_b"],
                  p["sg_w_eff"][sgu_block_rows], p["sg_b_eff"][sgu_block_rows], sgu_block_rows, want_vn)
    sg = sg_out[0]
    merged = _merge(att.astype(BF16), sg, proj, 3 * att_w + 2 * d_sg, 3 * att_w + 2 * d_sg + d,
                    p["w_pa"], p["w_pb"], tm, 512)
    x1 = _outproj(merged, p["w_o"], x, mod, 2, rows, tm, 512)
    h2 = _normmod(x1, p["norm2_g"], mod, 4, 3, rows, tm_row)
    qp = _matmul(h2, p["peer_wq"], tm, 512, "peer_q")
    tt = _tile(m, 512)
    tt_route = _tile(m, 1024)
    if "peer_u_bf" in p:
        route = _peer_route(qp, p["peer_subkeys"], tt_route)
    else:
        *route, p["peer_u_bf"], p["peer_v_bf"] = _peer_route(qp, p["peer_subkeys"], tt_route,
                                                            (p["peer_u"], p["peer_v"]))
    peer = _peer_dense(h2, p["peer_u_bf"], p["peer_v_bf"], route, tt, _tile(p["peer_u_bf"].shape[0], 512))
    y = _final(x1, peer, mod, 5, p["final_g"], rows, tm_row)
    return y, proj, (sg_out[1] if want_vn else None)


def kernel(x_prompt, x_sample, cache_k0, cache_v0, cache_k1, cache_v1, cache_k2, cache_v2, c_prompt, c_sample,
           ada_w, ada_b, norm1_g, norm2_g, w_in, sg_ln_g, sg_ln_b, sg_w, sg_b, w_pa, w_pb, w_o, peer_wq,
           peer_subkeys, peer_u, peer_v, final_g):
    batch, seq, d = x_prompt.shape
    dec_batch, dec_seq, _ = x_sample.shape
    depth = ada_w.shape[0]
    assert depth == 1, "single-layer step"
    att_w = N_HEADS_ATT * HEAD_DIM
    hw = HEADS_PER_GROUP * HEAD_DIM

    slopes = jnp.exp2(-8.0 * jnp.arange(1, N_HEADS_ATT + 1, dtype=F32) / N_HEADS_ATT).reshape(N_GROUPS, HEADS_PER_GROUP)
    sg_wl, sg_bl = sg_w[0], sg_b[0]
    reps = CHUNK // dec_seq
    p = dict(norm1_g=norm1_g[0], norm2_g=norm2_g[0], w_in=w_in[0], sg_ln_g=sg_ln_g[0], sg_ln_b=sg_ln_b[0],
             w_pa=w_pa[0], w_pb=w_pb[0], w_o=w_o[0], peer_wq=peer_wq[0], peer_subkeys=peer_subkeys[0],
             peer_u=peer_u[0], peer_v=peer_v[0], final_g=final_g,
             sg_w_eff={CHUNK: sg_wl, dec_seq: jnp.tile(sg_wl[:, :dec_seq, :dec_seq], (1, reps, reps))},
             sg_b_eff={CHUNK: sg_bl.T, dec_seq: jnp.tile(sg_bl.T[:dec_seq], (reps, 1))})

    c_all = jnp.concatenate([c_prompt, c_sample], axis=0)
    mod = _modulation(c_all, ada_w[0], ada_b[0])
    mod_p = mod[:batch].reshape(batch, 1, N_MOD * d)
    mod_s = jnp.repeat(mod[batch:], dec_seq, axis=0)

    y_p, proj_p, _ = _layer(
        x_prompt.reshape(batch * seq, d), mod_p, _Rows(True, seq),
        lambda proj: _attn_prompt(proj, slopes, batch, seq), CHUNK, False, p)

    caches = [(cache_k0, cache_v0), (cache_k1, cache_v1), (cache_k2, cache_v2)]
    y_s, proj_s, vn_s = _layer(
        x_sample.reshape(dec_batch * dec_seq, d), mod_s, _Rows(False, dec_seq),
        lambda proj: _attn_sample(proj, slopes, caches, dec_batch, dec_seq), dec_seq, True, p)

    outs = [y_p.reshape(batch, seq, d), y_s.reshape(dec_batch, dec_seq, d)]
    kp = proj_p.reshape(batch, seq, -1)
    ks = proj_s.reshape(dec_batch, dec_seq, -1)
    for g in range(N_GROUPS):
        keep = min(WINDOWS[g], seq)
        for src, rows_from in ((kp, seq - keep), (ks, 0)):
            for part in (1, 2):
                c0 = part * att_w + g * hw
                blk = src[:, rows_from:, c0:c0 + hw]
                outs.append(blk.reshape(1, blk.shape[0], blk.shape[1], HEADS_PER_GROUP, HEAD_DIM))
    outs.append(vn_s.reshape(1, dec_batch, dec_seq, d // 2))
    return tuple(outs)
```

```python
import functools
import math

import jax
import jax.numpy as jnp
from jax import lax
from jax.experimental import pallas as pl
from jax.experimental.pallas import tpu as pltpu

F32 = jnp.float32
BF16 = jnp.bfloat16
EPS = 1e-6
GATE_DTYPE = jnp.float32

VMEM_LIMIT_BYTES = 58 * 1024 * 1024
LANES = 128

HEAD_DIM = 128
N_GROUPS = 3
HEADS_PER_GROUP = 8
N_HEADS_ATT = N_GROUPS * HEADS_PER_GROUP
WINDOWS = (128, 512, 2048)
DILATIONS = (1, 4, 16)
ATT_SPAN = 128
CHUNK = 128
SG_GROUPS = 8
PEER_HEADS = 8
PEER_NKEYS = 128
PEER_TOPK = 16
N_MOD = 6

NT_DIMS = (((1,), (1,)), ((), ()))


def _params(*sem):
    return pltpu.CompilerParams(dimension_semantics=sem, vmem_limit_bytes=VMEM_LIMIT_BYTES)


def _gelu(x):
    return x * (lax.erf(x * (1.0 / math.sqrt(2.0))) + 1.0) * 0.5


class _Rows:
    def __init__(self, shared, rows_per_batch):
        self.shared = shared
        self.rows_per_batch = rows_per_batch

    def spec(self, tm, tn, col_block0=0):
        if self.shared:
            tpb = self.rows_per_batch // tm
            return pl.BlockSpec((None, 1, tn), lambda i, j: (i // tpb, 0, j + col_block0))
        return pl.BlockSpec((tm, tn), lambda i, j: (i, j + col_block0))


def _mod_kernel(c_ref, w_ref, b_ref, o_ref):
    c = c_ref[...]
    a = (c * jax.nn.sigmoid(c)).astype(BF16)
    o_ref[...] = jnp.dot(a, w_ref[...].astype(BF16), preferred_element_type=F32) + b_ref[...]


def _modulation(c, w, b, tn=1024):
    n, d = c.shape
    cols = w.shape[1]
    return pl.pallas_call(
        _mod_kernel,
        grid=(cols // tn,),
        in_specs=[pl.BlockSpec((n, d), lambda j: (0, 0)),
                  pl.BlockSpec((d, tn), lambda j: (0, j)),
                  pl.BlockSpec((1, tn), lambda j: (0, j))],
        out_specs=pl.BlockSpec((n, tn), lambda j: (0, j)),
        out_shape=jax.ShapeDtypeStruct((n, cols), F32),
        compiler_params=_params("parallel"),
        name="modulation",
    )(c, w, b.reshape(1, cols))


def _normmod_kernel(x_ref, g_ref, sc_ref, sh_ref, o_ref):
    x = x_ref[...]
    y = x * lax.rsqrt(jnp.mean(x * x, axis=-1, keepdims=True) + EPS)
    o_ref[...] = ((y * g_ref[...]) * (1.0 + sc_ref[...]) + sh_ref[...]).astype(o_ref.dtype)


def _normmod(x, g, mod, sc_idx, sh_idx, rows, tm):
    m, d = x.shape
    return pl.pallas_call(
        _normmod_kernel,
        grid=(m // tm, 1),
        in_specs=[pl.BlockSpec((tm, d), lambda i, j: (i, 0)),
                  pl.BlockSpec((1, d), lambda i, j: (0, 0)),
                  rows.spec(tm, d, sc_idx),
                  rows.spec(tm, d, sh_idx)],
        out_specs=pl.BlockSpec((tm, d), lambda i, j: (i, 0)),
        out_shape=jax.ShapeDtypeStruct((m, d), BF16),
        compiler_params=_params("parallel", "arbitrary"),
        name="normmod",
    )(x, g.reshape(1, d), mod, mod)


def _mm_kernel(a_ref, w_ref, o_ref):
    o_ref[...] = jnp.dot(a_ref[...], w_ref[...].astype(BF16), preferred_element_type=F32).astype(o_ref.dtype)


def _matmul(a, w, tm, tn, name):
    m, k = a.shape
    n = w.shape[1]
    return pl.pallas_call(
        _mm_kernel,
        grid=(m // tm, n // tn),
        in_specs=[pl.BlockSpec((tm, k), lambda i, j: (i, 0)),
                  pl.BlockSpec((k, tn), lambda i, j: (0, j))],
        out_specs=pl.BlockSpec((tm, tn), lambda i, j: (i, j)),
        out_shape=jax.ShapeDtypeStruct((m, n), F32),
        compiler_params=_params("parallel", "arbitrary"),
        name=name,
    )(a, w)


def _merge_kernel(att_ref, sg_ref, ga_ref, gb_ref, wa_ref, wb_ref, o_ref):
    pa = jnp.dot(att_ref[...], wa_ref[...].astype(BF16), preferred_element_type=F32)
    pb = jnp.dot(sg_ref[...], wb_ref[...].astype(BF16), preferred_element_type=F32)
    o_ref[...] = (jax.nn.sigmoid(ga_ref[...]) * pa + jax.nn.sigmoid(gb_ref[...]) * pb).astype(o_ref.dtype)


def _merge(att, sg, proj, ga_col, gb_col, w_pa, w_pb, tm, tn):
    m, ka = att.shape
    kb = sg.shape[1]
    n = w_pa.shape[1]
    ga0, gb0 = ga_col // tn, gb_col // tn
    return pl.pallas_call(
        _merge_kernel,
        grid=(m // tm, n // tn),
        in_specs=[pl.BlockSpec((tm, ka), lambda i, j: (i, 0)),
                  pl.BlockSpec((tm, kb), lambda i, j: (i, 0)),
                  pl.BlockSpec((tm, tn), lambda i, j: (i, j + ga0)),
                  pl.BlockSpec((tm, tn), lambda i, j: (i, j + gb0)),
                  pl.BlockSpec((ka, tn), lambda i, j: (0, j)),
                  pl.BlockSpec((kb, tn), lambda i, j: (0, j))],
        out_specs=pl.BlockSpec((tm, tn), lambda i, j: (i, j)),
        out_shape=jax.ShapeDtypeStruct((m, n), BF16),
        compiler_params=_params("parallel", "arbitrary"),
        name="merge",
    )(att, sg, proj, proj, w_pa, w_pb)


def _outproj_kernel(a_ref, w_ref, x_ref, gt_ref, o_ref):
    acc = jnp.dot(a_ref[...], w_ref[...].astype(BF16), preferred_element_type=F32)
    o_ref[...] = x_ref[...] + gt_ref[...] * acc


def _outproj(a, w, x, mod, gt_idx, rows, tm, tn):
    m, k = a.shape
    n = w.shape[1]
    return pl.pallas_call(
        _outproj_kernel,
        grid=(m // tm, n // tn),
        in_specs=[pl.BlockSpec((tm, k), lambda i, j: (i, 0)),
                  pl.BlockSpec((k, tn), lambda i, j: (0, j)),
                  pl.BlockSpec((tm, tn), lambda i, j: (i, j)),
                  rows.spec(tm, tn, gt_idx * (n // tn))],
        out_specs=pl.BlockSpec((tm, tn), lambda i, j: (i, j)),
        out_shape=jax.ShapeDtypeStruct((m, n), F32),
        compiler_params=_params("parallel", "arbitrary"),
        name="outproj",
    )(a, w, x, mod)


def _softmax_parts(s):
    m = jnp.max(s, axis=-1, keepdims=True)
    p = jnp.exp(s - m)
    return m, p, jnp.sum(p, axis=-1, keepdims=True)


def _combine_groups(outs, lses):
    top = functools.reduce(jnp.maximum, lses)
    es = [jnp.exp(l - top) for l in lses]
    tot = functools.reduce(jnp.add, es)
    return functools.reduce(jnp.add, [(e / tot) * o for e, o in zip(es, outs)])


def _attn_prompt_kernel(sl_ref, *refs, seq):
    q_refs, k_refs, v_refs = refs[0:9:3], refs[1:9:3], refs[2:9:3]
    o_ref, og_ref, lg_ref = refs[9], refs[10], refs[11]
    h = pl.program_id(1)
    scale = HEAD_DIM ** -0.5
    span = ATT_SPAN
    back1 = lax.broadcasted_iota(jnp.int32, (span, span), 0) - lax.broadcasted_iota(jnp.int32, (span, span), 1)
    back2 = (lax.broadcasted_iota(jnp.int32, (span, 2 * span), 0) + span
             - lax.broadcasted_iota(jnp.int32, (span, 2 * span), 1))
    for g in range(N_GROUPS):
        dil = DILATIONS[g]
        n_blocks = seq // dil // span
        step_bias = sl_ref[g, h] * float(dil)
        bias1 = jnp.where(back1 >= 0, -(step_bias * back1.astype(F32)), -jnp.inf)
        bias2 = jnp.where((back2 >= 0) & (back2 <= span), -(step_bias * back2.astype(F32)), -jnp.inf)
        blocks = [(res, blk) for blk in range(n_blocks) for res in range(dil)]
        n_first = dil
        rows = [pl.ds(res + dil * span * blk, span) if dil == 1 else pl.ds(res + dil * span * blk, span, stride=dil)
                for res, blk in blocks]
        q = jnp.stack([q_refs[g][r, :] for r in rows]).astype(BF16)
        k_cur = [k_refs[g][r, :].astype(BF16) for r in rows]
        v_cur = [v_refs[g][r, :].astype(BF16) for r in rows]
        if n_blocks == 1:
            k, v = jnp.stack(k_cur), jnp.stack(v_cur)
            bias = bias1[None]
        else:
            prev = [i if blk == 0 else blocks.index((res, blk - 1)) for i, (res, blk) in enumerate(blocks)]
            k = jnp.stack([jnp.concatenate([k_cur[j], k_cur[i]], axis=0) for i, j in enumerate(prev)])
            v = jnp.stack([jnp.concatenate([v_cur[j], v_cur[i]], axis=0) for i, j in enumerate(prev)])
            bias_first = jnp.concatenate([jnp.full((span, span), -jnp.inf, F32), bias1], axis=1)
            bias = jnp.concatenate([jnp.broadcast_to(bias_first[None], (n_first, span, 2 * span)),
                                    jnp.broadcast_to(bias2[None], (len(blocks) - n_first, span, 2 * span))], axis=0)
        s = lax.dot_general(q, k, (((2,), (2,)), ((0,), (0,))), preferred_element_type=F32) * scale + bias
        m, p, den = _softmax_parts(s)
        o = lax.dot_general((p / den).astype(BF16), v, (((2,), (1,)), ((0,), (0,))), preferred_element_type=F32)
        lse = m + jnp.log(den)
        for i, r in enumerate(rows):
            og_ref[g, r, :] = o[i]
            lg_ref[g, r, :] = jnp.broadcast_to(lse[i], (span, HEAD_DIM))

    def combine(i, carry):
        rows = pl.ds(pl.multiple_of(i * span, span), span)
        o_ref[rows, :] = _combine_groups([og_ref[g, rows, :] for g in range(N_GROUPS)],
                                         [lg_ref[g, rows, :] for g in range(N_GROUPS)]).astype(o_ref.dtype)
        return carry

    lax.fori_loop(0, seq // span, combine, 0)


def _attn_prompt(qkv, slopes, batch, seq):
    ncols = qkv.shape[1]
    assert seq % (max(DILATIONS) * ATT_SPAN) == 0, "every residue class needs whole query blocks"
    qkv3 = qkv.reshape(batch, seq, ncols)
    in_specs = [pl.BlockSpec(memory_space=pltpu.SMEM)]
    for g in range(N_GROUPS):
        for part in range(3):
            cb = part * N_HEADS_ATT + g * HEADS_PER_GROUP
            in_specs.append(pl.BlockSpec((None, seq, HEAD_DIM), lambda b, h, cb=cb: (b, 0, cb + h)))
    out = pl.pallas_call(
        functools.partial(_attn_prompt_kernel, seq=seq),
        grid=(batch, HEADS_PER_GROUP),
        in_specs=in_specs,
        out_specs=pl.BlockSpec((None, seq, HEAD_DIM), lambda b, h: (b, 0, h)),
        out_shape=jax.ShapeDtypeStruct((batch, seq, HEADS_PER_GROUP * HEAD_DIM), BF16),
        scratch_shapes=[pltpu.VMEM((N_GROUPS, seq, HEAD_DIM), F32),
                        pltpu.VMEM((N_GROUPS, seq, HEAD_DIM), F32)],
        compiler_params=_params("parallel", "arbitrary"),
        name="attn_prompt",
    )(slopes, *([qkv3] * 9))
    return out.reshape(batch * seq, HEADS_PER_GROUP * HEAD_DIM)


def _attn_sample_kernel(sl_ref, qkv_ref, k0, v0, k1, v1, k2, v2, o_ref, *, dec_seq):
    scale = HEAD_DIM ** -0.5
    att_w = N_HEADS_ATT * HEAD_DIM
    nh = HEADS_PER_GROUP
    caches = ((k0, v0), (k1, v1), (k2, v2))
    batch_nt = (((2,), (2,)), ((0,), (0,)))
    batch_nn = (((2,), (1,)), ((0,), (0,)))
    t_new = lax.broadcasted_iota(jnp.int32, (dec_seq, dec_seq), 0)
    c_new = lax.broadcasted_iota(jnp.int32, (dec_seq, dec_seq), 1)
    outs, lses = [], []
    for g in range(N_GROUPS):
        dil, win = DILATIONS[g], WINDOWS[g]

        def new_rows(part):
            cols = [part * att_w + (g * nh + h) * HEAD_DIM for h in range(nh)]
            return jnp.stack([qkv_ref[:, c:c + HEAD_DIM] for c in cols]).astype(BF16)

        def cached(ref):
            per_head = []
            for h in range(nh):
                head_rows = pl.ds(h, ref.shape[-2] // nh, stride=nh)
                per_head.append(ref[:, head_rows, :].reshape(-1, HEAD_DIM) if g == 2 else ref[head_rows, :])
            return jnp.stack(per_head).astype(BF16)

        q, kn, vn = new_rows(0), new_rows(1), new_rows(2)
        kc, vc = cached(caches[g][0]), cached(caches[g][1])
        rows = kc.shape[1]
        t_c = lax.broadcasted_iota(jnp.int32, (dec_seq, rows), 0)
        c_c = lax.broadcasted_iota(jnp.int32, (dec_seq, rows), 1)
        e_c = ((c_c >> 3) << 4) + (c_c & 7) if g == 2 else c_c
        d_c = win + t_c - e_c
        ok_c = ((d_c & (dil - 1)) == 0) & (d_c <= ATT_SPAN * dil)
        d_n = t_new - c_new
        ok_n = (d_n >= 0) & ((d_n & (dil - 1)) == 0)
        slope = jnp.stack([jnp.full((1, 1), sl_ref[g, h], F32) for h in range(nh)])
        s_c = lax.dot_general(q, kc, batch_nt, preferred_element_type=F32) * scale
        s_n = lax.dot_general(q, kn, batch_nt, preferred_element_type=F32) * scale
        s_c = jnp.where(ok_c[None], s_c - slope * d_c.astype(F32)[None], -jnp.inf)
        s_n = jnp.where(ok_n[None], s_n - slope * d_n.astype(F32)[None], -jnp.inf)
        m = jnp.maximum(jnp.max(s_c, axis=-1, keepdims=True), jnp.max(s_n, axis=-1, keepdims=True))
        p_c = jnp.exp(s_c - m)
        p_n = jnp.exp(s_n - m)
        den = jnp.sum(p_c, axis=-1, keepdims=True) + jnp.sum(p_n, axis=-1, keepdims=True)
        outs.append(lax.dot_general((p_c / den).astype(BF16), vc, batch_nn, preferred_element_type=F32)
                    + lax.dot_general((p_n / den).astype(BF16), vn, batch_nn, preferred_element_type=F32))
        lses.append(m + jnp.log(den))
    combined = _combine_groups(outs, lses)
    for h in range(nh):
        o_ref[:, h * HEAD_DIM:(h + 1) * HEAD_DIM] = combined[h]


def _attn_sample(qkv, slopes, caches, dec_batch, dec_seq):
    att_w = N_HEADS_ATT * HEAD_DIM
    hw = HEADS_PER_GROUP * HEAD_DIM
    ins, specs = [], []
    for g in range(N_GROUPS):
        for c in caches[g]:
            win = c.shape[2]
            if g == 2:
                dil = DILATIONS[g]
                assert dec_seq <= dil and dec_seq % 8 == 0
                ins.append(c.reshape(1, dec_batch, win // dil, dil * HEADS_PER_GROUP, HEAD_DIM))
                specs.append(pl.BlockSpec((None, None, win // dil, dec_seq * HEADS_PER_GROUP, HEAD_DIM),
                                          lambda b: (0, b, 0, 0, 0)))
            else:
                ins.append(c.reshape(1, dec_batch, win * HEADS_PER_GROUP, HEAD_DIM))
                specs.append(pl.BlockSpec((None, None, win * HEADS_PER_GROUP, HEAD_DIM), lambda b: (0, b, 0, 0)))
    return pl.pallas_call(
        functools.partial(_attn_sample_kernel, dec_seq=dec_seq),
        grid=(dec_batch,),
        in_specs=[pl.BlockSpec(memory_space=pltpu.SMEM),
                  pl.BlockSpec((dec_seq, 3 * att_w), lambda b: (b, 0))] + specs,
        out_specs=pl.BlockSpec((dec_seq, hw), lambda b: (b, 0)),
        out_shape=jax.ShapeDtypeStruct((dec_batch * dec_seq, hw), F32),
        compiler_params=_params("parallel"),
        name="attn_sample",
    )(slopes, qkv, *ins)


def _sgu_kernel(*refs, block_rows, parts, want_vn):
    su_refs, sv_refs = refs[:parts], refs[parts:2 * parts]
    lg_ref, lb_ref, w_ref, b_ref, sg_ref = refs[2 * parts:2 * parts + 5]

    def whole(rs):
        return rs[0][...] if parts == 1 else jnp.concatenate([r[...] for r in rs], axis=1)

    u = _gelu(whole(su_refs))
    v = _gelu(whole(sv_refs))
    mu = jnp.mean(v, axis=-1, keepdims=True)
    vc = v - mu
    vn = vc * lax.rsqrt(jnp.mean(vc * vc, axis=-1, keepdims=True) + EPS) * lg_ref[...] + lb_ref[...]
    if want_vn:
        refs[2 * parts + 5][...] = vn
    r = lax.broadcasted_iota(jnp.int32, (CHUNK, CHUNK), 0)
    c = lax.broadcasted_iota(jnp.int32, (CHUNK, CHUNK), 1)
    keep = (c <= r) & ((r // block_rows) == (c // block_rows))
    ch = vn.shape[1] // SG_GROUPS
    for g in range(SG_GROUPS):
        w = jnp.where(keep, w_ref[g], 0.0).astype(BF16)
        mixed = jnp.dot(w, vn[:, g * ch:(g + 1) * ch].astype(BF16), preferred_element_type=F32) + b_ref[:, g:g + 1]
        sg_ref[:, g * ch:(g + 1) * ch] = (u[:, g * ch:(g + 1) * ch] * mixed).astype(sg_ref.dtype)


def _sgu(proj, su_col, sv_col, d_sg, ln_g, ln_b, w, bias_t, block_rows, want_vn):
    m = proj.shape[0]
    out_shape = [jax.ShapeDtypeStruct((m, d_sg), BF16)]
    out_specs = [pl.BlockSpec((CHUNK, d_sg), lambda i: (i, 0))]
    if want_vn:
        out_shape.append(jax.ShapeDtypeStruct((m, d_sg), F32))
        out_specs.append(pl.BlockSpec((CHUNK, d_sg), lambda i: (i, 0)))
    unit = math.gcd(math.gcd(su_col, sv_col), d_sg)
    parts = d_sg // unit
    col_specs = [pl.BlockSpec((CHUNK, unit), lambda i, cb=c0 // unit + k: (i, cb))
                 for c0 in (su_col, sv_col) for k in range(parts)]
    return pl.pallas_call(
        functools.partial(_sgu_kernel, block_rows=block_rows, parts=parts, want_vn=want_vn),
        grid=(m // CHUNK,),
        in_specs=col_specs + [pl.BlockSpec((1, d_sg), lambda i: (0, 0)),
                              pl.BlockSpec((1, d_sg), lambda i: (0, 0)),
                              pl.BlockSpec((SG_GROUPS, CHUNK, CHUNK), lambda i: (0, 0, 0)),
                              pl.BlockSpec((CHUNK, SG_GROUPS), lambda i: (0, 0))],
        out_specs=out_specs,
        out_shape=out_shape,
        compiler_params=_params("parallel"),
        name="sgu",
    )(*([proj] * (2 * parts)), ln_g.reshape(1, d_sg), ln_b.reshape(1, d_sg), w, bias_t)


def _take_top(s, key_id, n_take):
    rows, cols = s.shape
    slot = lax.broadcasted_iota(jnp.int32, (n_take, cols), 0)
    rank = jnp.full((rows, cols), float(n_take), F32)
    top = jnp.zeros((n_take, cols), F32)
    for r in range(n_take):
        m = jnp.max(s, axis=0, keepdims=True)
        first = jnp.min(jnp.where(s == m, key_id, float(rows)), axis=0, keepdims=True)
        sel = key_id == first
        rank = jnp.where(sel, float(r), rank)
        s = jnp.where(sel, -jnp.inf, s)
        top = jnp.where(slot == r, m, top)
    return top, rank


MARK = 2.0 ** 100
MARK_STEP = 2.0 ** 93
MARK_FLOOR = -(2.0 ** 90)


def _take_top_unique(s, n_take):
    cols = s.shape[1]
    slot = lax.broadcasted_iota(jnp.int32, (n_take, cols), 0)
    top = jnp.zeros((n_take, cols), F32)
    for r in range(n_take):
        m = jnp.max(s, axis=0, keepdims=True)
        top = jnp.where(slot == r, m, top)
        s = jnp.where(s == m, -(MARK + r * MARK_STEP), s)
    removed = s <= -MARK
    rank = jnp.where(removed, (-s - MARK) * (1.0 / MARK_STEP), float(n_take))
    return top, rank, jnp.sum(jnp.where(removed, 1.0, 0.0), axis=0, keepdims=True)


def _candidate_pieces(top0, top1):
    tt = top0.shape[1]
    sub = lax.broadcasted_iota(jnp.int32, (8, tt), 0)
    subf = sub.astype(F32)
    pieces, flat = [], []
    for a in range(8):
        n_b = PEER_TOPK // (a + 1)
        for b0 in range(0, n_b, 8):
            c = top0[a:a + 1, :] + top1[b0:b0 + 8, :]
            pieces.append(jnp.where(sub < n_b - b0, c, -jnp.inf))
            flat.append(subf + float(a * PEER_TOPK + b0))
    pieces.append(top0[8:16, :] + top1[0:1, :])
    flat.append((subf + 8.0) * float(PEER_TOPK))
    return pieces, flat


def _write_route(scores, top0, top1, rank0, rank1, pieces, taken, rk_ref, e1_ref, lk_ref, e0_ref):
    nk, tt = rank0.shape
    best = top0[0:1, :] + top1[0:1, :]
    z = functools.reduce(jnp.add, [jnp.sum(jnp.where(t > 0.0, jnp.exp(c - best), 0.0), axis=0, keepdims=True)
                                   for t, c in zip(taken, pieces)])
    counts = [jnp.sum(taken[0] + taken[1], axis=0, keepdims=True)]
    counts += [jnp.sum(taken[a + 1], axis=0, keepdims=True) for a in range(1, 8)]
    counts += [taken[-1][a:a + 1, :] for a in range(8)]
    lk = jnp.zeros((nk, tt), F32)
    for a in range(PEER_TOPK):
        lk = jnp.where(rank0 == float(a), counts[a], lk)
    rk_ref[...] = rank1.astype(rk_ref.dtype)
    e1_ref[...] = jnp.exp(scores[1] - top1[0:1, :]).astype(e1_ref.dtype)
    lk_ref[...] = lk
    e0_ref[...] = jnp.exp(scores[0] - top0[0:1, :]) / z


def _peer_route_kernel(q_ref, sk_ref, *refs):
    tt = q_ref.shape[0]
    half = q_ref.shape[1] // 2
    nk = sk_ref.shape[1]
    if len(refs) > 4:
        t0_ref, t1_ref, c0_ref, c1_ref = refs[0], refs[1], refs[6], refs[7]
        c0_ref[...] = t0_ref[...].astype(BF16)
        c1_ref[...] = t1_ref[...].astype(BF16)
        refs = refs[2:6]
    out_refs = refs
    scores = [lax.dot_general(sk_ref[p].astype(BF16), q_ref[:, p * half:(p + 1) * half].astype(BF16),
                              NT_DIMS, preferred_element_type=F32) for p in range(2)]

    top0, rank0, removed0 = _take_top_unique(scores[0], PEER_TOPK)
    top1, rank1, removed1 = _take_top_unique(scores[1], PEER_TOPK)
    pieces, _ = _candidate_pieces(top0, top1)
    cand = list(pieces)
    for _ in range(PEER_TOPK):
        m = jnp.max(functools.reduce(jnp.maximum, cand), axis=0, keepdims=True)
        cand = [jnp.where(c == m, -MARK, c) for c in cand]
    taken = [jnp.where(c == -MARK, 1.0, 0.0) for c in cand]
    n_taken = functools.reduce(jnp.add, [jnp.sum(t, axis=0, keepdims=True) for t in taken])
    lowest = jnp.minimum(jnp.min(scores[0], axis=0, keepdims=True), jnp.min(scores[1], axis=0, keepdims=True))
    k = float(PEER_TOPK)
    unsure = (removed0 != k) | (removed1 != k) | (n_taken != k) | (lowest < MARK_FLOOR)
    _write_route(scores, top0, top1, rank0, rank1, pieces, taken, *out_refs)

    @pl.when(jnp.sum(jnp.where(unsure, 1.0, 0.0)) > 0.0)
    def _():
        key_id = lax.broadcasted_iota(jnp.int32, (nk, tt), 0).astype(F32)
        xtop0, xrank0 = _take_top(scores[0], key_id, PEER_TOPK)
        xtop1, xrank1 = _take_top(scores[1], key_id, PEER_TOPK)
        xpieces, flat = _candidate_pieces(xtop0, xtop1)
        xcand = list(xpieces)
        xtaken = [jnp.zeros((8, tt), F32) for _ in xpieces]
        big = float(PEER_TOPK * PEER_TOPK)
        for _ in range(PEER_TOPK):
            m = jnp.max(functools.reduce(jnp.maximum, xcand), axis=0, keepdims=True)
            first = jnp.min(functools.reduce(jnp.minimum,
                                             [jnp.where(c == m, f, big) for c, f in zip(xcand, flat)]),
                            axis=0, keepdims=True)
            for i, f in enumerate(flat):
                sel = f == first
                xcand[i] = jnp.where(sel, -jnp.inf, xcand[i])
                xtaken[i] = jnp.where(sel, 1.0, xtaken[i])
        _write_route(scores, xtop0, xtop1, xrank0, xrank1, xpieces, xtaken, *out_refs)


def _peer_route(qp, sub_keys, tt, tables=()):
    m = qp.shape[0]
    nh, _, nk, half = sub_keys.shape
    ospec = pl.BlockSpec((None, nk, tt), lambda i, h: (h, 0, i))
    in_specs = [pl.BlockSpec((tt, 2 * half), lambda i, h: (i, h)),
                pl.BlockSpec((None, 2, nk, half), lambda i, h: (h, 0, 0, 0))]
    out_specs = [ospec] * 4
    out_shape = [jax.ShapeDtypeStruct((nh, nk, m), dt) for dt in (GATE_DTYPE, GATE_DTYPE, F32, F32)]
    if tables:
        rows, width = tables[0].shape
        n_steps = (m // tt) * nh
        rb = rows // n_steps
        assert rows % n_steps == 0 and rb % 16 == 0, (rows, n_steps)
        tspec = pl.BlockSpec((rb, width), lambda i, h: (i * nh + h, 0))
        in_specs += [tspec, tspec]
        out_specs += [tspec, tspec]
        out_shape += [jax.ShapeDtypeStruct((rows, width), BF16)] * 2
    return pl.pallas_call(
        _peer_route_kernel,
        grid=(m // tt, nh),
        in_specs=in_specs,
        out_specs=out_specs,
        out_shape=out_shape,
        compiler_params=_params("arbitrary", "arbitrary"),
        name="peer_route",
    )(qp, sub_keys, *tables)


MXU_COLS = 256


def _peer_dense_kernel(h_ref, u_ref, v_ref, rk_ref, e1_ref, lk_ref, e0_ref, o_ref, act_ref, g_ref, *,
                       n_chunks, n_items):
    s = pl.program_id(0)
    _, tt, ec = act_ref.shape
    nk = rk_ref.shape[1]
    d = v_ref.shape[1]
    first_key = (jnp.clip(s - 1, 0, n_items - 1) % n_chunks) * (ec // nk)
    down_chunk = jnp.clip(s - 2, 0, n_items - 1) % n_chunks

    @pl.when(s == 0)
    def _():
        act_ref[...] = jnp.zeros(act_ref.shape, act_ref.dtype)
        g_ref[...] = jnp.zeros(g_ref.shape, g_ref.dtype)

    @pl.when(down_chunk == 0)
    def _():
        o_ref[...] = jnp.zeros(o_ref.shape, o_ref.dtype)

    n_tiles = d // MXU_COLS
    pieces = [(ii, tb) for ii in range(ec // nk) for tb in range(tt // nk)]
    per_tile = -(-len(pieces) // n_tiles)
    half = ec // 2

    def step(rs, ws):
        key_rows = {}

        def key_row(ref, hd, ii):
            if (id(ref), hd, ii) not in key_rows:
                key_rows[id(ref), hd, ii] = ref[hd, pl.ds(first_key + ii, 1), :].astype(GATE_DTYPE)
            return key_rows[id(ref), hd, ii]

        for n in range(n_tiles):
            if n == 0:
                act_ref[ws, :, :half] = lax.dot_general(h_ref[...], u_ref[:half, :], NT_DIMS,
                                                        preferred_element_type=F32)
            if n == n_tiles // 2:
                act_ref[ws, :, half:] = lax.dot_general(h_ref[...], u_ref[half:, :], NT_DIMS,
                                                        preferred_element_type=F32)
            cols = slice(n * MXU_COLS, (n + 1) * MXU_COLS)
            o_ref[:, cols] += jnp.dot(g_ref[rs], v_ref[:, cols], preferred_element_type=F32)
            for ii, tb in pieces[n * per_tile:(n + 1) * per_tile]:
                toks = slice(tb * nk, (tb + 1) * nk)
                exps = slice(ii * nk, (ii + 1) * nk)
                w = None
                for hd in range(rk_ref.shape[0]):
                    take = rk_ref[hd, :, toks] < key_row(lk_ref, hd, ii)[:, toks]
                    term = (jnp.where(take, e1_ref[hd, :, toks], jnp.zeros((), GATE_DTYPE))
                            * key_row(e0_ref, hd, ii)[:, toks])
                    w = term if w is None else w + term
                g_ref[ws, toks, exps] = (w.astype(F32).T * _gelu(act_ref[rs, toks, exps])).astype(BF16)

    @pl.when(s % 2 == 0)
    def _():
        step(1, 0)

    @pl.when(s % 2 == 1)
    def _():
        step(0, 1)


def _peer_dense(h2, u_bf, v_bf, route, tt, ec):
    m, d = h2.shape
    n_chunks = u_bf.shape[0] // ec
    n_items = (m // tt) * n_chunks
    nh, nk, _ = route[0].shape

    def up_item(s):
        return jnp.minimum(s, n_items - 1)

    def gate_item(s):
        return jnp.clip(s - 1, 0, n_items - 1)

    def down_item(s):
        return jnp.clip(s - 2, 0, n_items - 1)

    rspec = pl.BlockSpec((nh, nk, tt), lambda s: (0, 0, gate_item(s) // n_chunks), pipeline_mode=pl.Buffered(1))
    return pl.pallas_call(
        functools.partial(_peer_dense_kernel, n_chunks=n_chunks, n_items=n_items),
        grid=(n_items + 2,),
        in_specs=[pl.BlockSpec((tt, d), lambda s: (up_item(s) // n_chunks, 0), pipeline_mode=pl.Buffered(1)),
                  pl.BlockSpec((ec, d), lambda s: (up_item(s) % n_chunks, 0)),
                  pl.BlockSpec((ec, d), lambda s: (down_item(s) % n_chunks, 0)),
                  rspec, rspec, rspec, rspec],
        out_specs=pl.BlockSpec((tt, d), lambda s: (down_item(s) // n_chunks, 0)),
        out_shape=jax.ShapeDtypeStruct((m, d), F32),
        scratch_shapes=[pltpu.VMEM((2, tt, ec), F32), pltpu.VMEM((2, tt, ec), BF16)],
        compiler_params=_params("arbitrary"),
        name="peer_dense",
    )(h2, u_bf, v_bf, *route)


def _final_kernel(x_ref, p_ref, gt_ref, g_ref, o_ref):
    x = x_ref[...] + gt_ref[...] * p_ref[...]
    o_ref[...] = x * lax.rsqrt(jnp.mean(x * x, axis=-1, keepdims=True) + EPS) * g_ref[...]


def _final(x, peer, mod, gt_idx, g, rows, tm):
    m, d = x.shape
    return pl.pallas_call(
        _final_kernel,
        grid=(m // tm, 1),
        in_specs=[pl.BlockSpec((tm, d), lambda i, j: (i, 0)),
                  pl.BlockSpec((tm, d), lambda i, j: (i, 0)),
                  rows.spec(tm, d, gt_idx),
                  pl.BlockSpec((1, d), lambda i, j: (0, 0))],
        out_specs=pl.BlockSpec((tm, d), lambda i, j: (i, 0)),
        out_shape=jax.ShapeDtypeStruct((m, d), F32),
        compiler_params=_params("parallel", "arbitrary"),
        name="final",
    )(x, peer, mod, g.reshape(1, d))


def _tile(n, want):
    t = min(n, want)
    assert n % t == 0, (n, want)
    return t


def _layer(x, mod, rows, attend, sgu_block_rows, want_vn, p):
    m, d = x.shape
    d_sg = d // 2
    att_w = N_HEADS_ATT * HEAD_DIM
    tm = _tile(m, 1024)
    tm_row = _tile(m, 256)
    h1 = _normmod(x, p["norm1_g"], mod, 1, 0, rows, tm_row)
    proj = _matmul(h1, p["w_in"], tm, 512, "inproj")
    att = attend(proj)
    sg_out = _sgu(proj, 3 * att_w, 3 * att_w + d_sg, d_sg, p["sg_ln_g"], p["sg_ln_b"],
                  p["sg_w_eff"][sgu_block_rows], p["sg_b_eff"][sgu_block_rows], sgu_block_rows, want_vn)
    sg = sg_out[0]
    merged = _merge(att.astype(BF16), sg, proj, 3 * att_w + 2 * d_sg, 3 * att_w + 2 * d_sg + d,
                    p["w_pa"], p["w_pb"], tm, 512)
    x1 = _outproj(merged, p["w_o"], x, mod, 2, rows, tm, 512)
    h2 = _normmod(x1, p["norm2_g"], mod, 4, 3, rows, tm_row)
    qp = _matmul(h2, p["peer_wq"], tm, 512, "peer_q")
    tt = _tile(m, 512)
    tt_route = _tile(m, 1024)
    if "peer_u_bf" in p:
        route = _peer_route(qp, p["peer_subkeys"], tt_route)
    else:
        *route, p["peer_u_bf"], p["peer_v_bf"] = _peer_route(qp, p["peer_subkeys"], tt_route,
                                                            (p["peer_u"], p["peer_v"]))
    peer = _peer_dense(h2, p["peer_u_bf"], p["peer_v_bf"], route, tt, _tile(p["peer_u_bf"].shape[0], 512))
    y = _final(x1, peer, mod, 5, p["final_g"], rows, tm_row)
    return y, proj, (sg_out[1] if want_vn else None)


def kernel(x_prompt, x_sample, cache_k0, cache_v0, cache_k1, cache_v1, cache_k2, cache_v2, c_prompt, c_sample,
           ada_w, ada_b, norm1_g, norm2_g, w_in, sg_ln_g, sg_ln_b, sg_w, sg_b, w_pa, w_pb, w_o, peer_wq,
           peer_subkeys, peer_u, peer_v, final_g):
    batch, seq, d = x_prompt.shape
    dec_batch, dec_seq, _ = x_sample.shape
    depth = ada_w.shape[0]
    assert depth == 1, "single-layer step"
    att_w = N_HEADS_ATT * HEAD_DIM
    hw = HEADS_PER_GROUP * HEAD_DIM

    slopes = jnp.exp2(-8.0 * jnp.arange(1, N_HEADS_ATT + 1, dtype=F32) / N_HEADS_ATT).reshape(N_GROUPS, HEADS_PER_GROUP)
    sg_wl, sg_bl = sg_w[0], sg_b[0]
    reps = CHUNK // dec_seq
    p = dict(norm1_g=norm1_g[0], norm2_g=norm2_g[0], w_in=w_in[0], sg_ln_g=sg_ln_g[0], sg_ln_b=sg_ln_b[0],
             w_pa=w_pa[0], w_pb=w_pb[0], w_o=w_o[0], peer_wq=peer_wq[0], peer_subkeys=peer_subkeys[0],
             peer_u=peer_u[0], peer_v=peer_v[0], final_g=final_g,
             sg_w_eff={CHUNK: sg_wl, dec_seq: jnp.tile(sg_wl[:, :dec_seq, :dec_seq], (1, reps, reps))},
             sg_b_eff={CHUNK: sg_bl.T, dec_seq: jnp.tile(sg_bl.T[:dec_seq], (reps, 1))})

    c_all = jnp.concatenate([c_prompt, c_sample], axis=0)
    mod = _modulation(c_all, ada_w[0], ada_b[0])
    mod_p = mod[:batch].reshape(batch, 1, N_MOD * d)
    mod_s = jnp.repeat(mod[batch:], dec_seq, axis=0)

    y_p, proj_p, _ = _layer(
        x_prompt.reshape(batch * seq, d), mod_p, _Rows(True, seq),
        lambda proj: _attn_prompt(proj, slopes, batch, seq), CHUNK, False, p)

    caches = [(cache_k0, cache_v0), (cache_k1, cache_v1), (cache_k2, cache_v2)]
    y_s, proj_s, vn_s = _layer(
        x_sample.reshape(dec_batch * dec_seq, d), mod_s, _Rows(False, dec_seq),
        lambda proj: _attn_sample(proj, slopes, caches, dec_batch, dec_seq), dec_seq, True, p)

    outs = [y_p.reshape(batch, seq, d), y_s.reshape(dec_batch, dec_seq, d)]
    kp = proj_p.reshape(batch, seq, -1)
    ks = proj_s.reshape(dec_batch, dec_seq, -1)
    for g in range(N_GROUPS):
        keep = min(WINDOWS[g], seq)
        for src, rows_from in ((kp, seq - keep), (ks, 0)):
            for part in (1, 2):
                c0 = part * att_w + g * hw
                blk = src[:, rows_from:, c0:c0 + hw]
                outs.append(blk.reshape(1, blk.shape[0], blk.shape[1], HEADS_PER_GROUP, HEAD_DIM))
    outs.append(vn_s.reshape(1, dec_batch, dec_seq, d // 2))
    return tuple(outs)
```

```python
import functools
import math

import jax
import jax.numpy as jnp
from jax import lax
from jax.experimental import pallas as pl
from jax.experimental.pallas import tpu as pltpu

F32 = jnp.float32
BF16 = jnp.bfloat16
EPS = 1e-6
GATE_DTYPE = jnp.float32

VMEM_LIMIT_BYTES = 58 * 1024 * 1024
LANES = 128

HEAD_DIM = 128
N_GROUPS = 3
HEADS_PER_GROUP = 8
N_HEADS_ATT = N_GROUPS * HEADS_PER_GROUP
WINDOWS = (128, 512, 2048)
DILATIONS = (1, 4, 16)
ATT_SPAN = 128
CHUNK = 128
SG_GROUPS = 8
PEER_HEADS = 8
PEER_NKEYS = 128
PEER_TOPK = 16
N_MOD = 6

NT_DIMS = (((1,), (1,)), ((), ()))


def _params(*sem):
    return pltpu.CompilerParams(dimension_semantics=sem, vmem_limit_bytes=VMEM_LIMIT_BYTES)


def _gelu(x):
    return x * (lax.erf(x * (1.0 / math.sqrt(2.0))) + 1.0) * 0.5


class _Rows:
    def __init__(self, shared, rows_per_batch):
        self.shared = shared
        self.rows_per_batch = rows_per_batch

    def spec(self, tm, tn, col_block0=0):
        if self.shared:
            tpb = self.rows_per_batch // tm
            return pl.BlockSpec((None, 1, tn), lambda i, j: (i // tpb, 0, j + col_block0))
        return pl.BlockSpec((tm, tn), lambda i, j: (i, j + col_block0))


def _mod_kernel(c_ref, w_ref, b_ref, o_ref):
    c = c_ref[...]
    a = (c * jax.nn.sigmoid(c)).astype(BF16)
    o_ref[...] = jnp.dot(a, w_ref[...].astype(BF16), preferred_element_type=F32) + b_ref[...]


def _modulation(c, w, b, tn=1024):
    n, d = c.shape
    cols = w.shape[1]
    return pl.pallas_call(
        _mod_kernel,
        grid=(cols // tn,),
        in_specs=[pl.BlockSpec((n, d), lambda j: (0, 0)),
                  pl.BlockSpec((d, tn), lambda j: (0, j)),
                  pl.BlockSpec((1, tn), lambda j: (0, j))],
        out_specs=pl.BlockSpec((n, tn), lambda j: (0, j)),
        out_shape=jax.ShapeDtypeStruct((n, cols), F32),
        compiler_params=_params("parallel"),
        name="modulation",
    )(c, w, b.reshape(1, cols))


def _normmod_kernel(x_ref, g_ref, sc_ref, sh_ref, o_ref):
    x = x_ref[...]
    y = x * lax.rsqrt(jnp.mean(x * x, axis=-1, keepdims=True) + EPS)
    o_ref[...] = ((y * g_ref[...]) * (1.0 + sc_ref[...]) + sh_ref[...]).astype(o_ref.dtype)


def _normmod(x, g, mod, sc_idx, sh_idx, rows, tm):
    m, d = x.shape
    return pl.pallas_call(
        _normmod_kernel,
        grid=(m // tm, 1),
        in_specs=[pl.BlockSpec((tm, d), lambda i, j: (i, 0)),
                  pl.BlockSpec((1, d), lambda i, j: (0, 0)),
                  rows.spec(tm, d, sc_idx),
                  rows.spec(tm, d, sh_idx)],
        out_specs=pl.BlockSpec((tm, d), lambda i, j: (i, 0)),
        out_shape=jax.ShapeDtypeStruct((m, d), BF16),
        compiler_params=_params("parallel", "arbitrary"),
        name="normmod",
    )(x, g.reshape(1, d), mod, mod)


def _mm_kernel(a_ref, w_ref, o_ref):
    o_ref[...] = jnp.dot(a_ref[...], w_ref[...].astype(BF16), preferred_element_type=F32).astype(o_ref.dtype)


def _matmul(a, w, tm, tn, name):
    m, k = a.shape
    n = w.shape[1]
    return pl.pallas_call(
        _mm_kernel,
        grid=(m // tm, n // tn),
        in_specs=[pl.BlockSpec((tm, k), lambda i, j: (i, 0)),
                  pl.BlockSpec((k, tn), lambda i, j: (0, j))],
        out_specs=pl.BlockSpec((tm, tn), lambda i, j: (i, j)),
        out_shape=jax.ShapeDtypeStruct((m, n), F32),
        compiler_params=_params("parallel", "arbitrary"),
        name=name,
    )(a, w)


def _merge_kernel(att_ref, sg_ref, ga_ref, gb_ref, wa_ref, wb_ref, o_ref):
    pa = jnp.dot(att_ref[...], wa_ref[...].astype(BF16), preferred_element_type=F32)
    pb = jnp.dot(sg_ref[...], wb_ref[...].astype(BF16), preferred_element_type=F32)
    o_ref[...] = (jax.nn.sigmoid(ga_ref[...]) * pa + jax.nn.sigmoid(gb_ref[...]) * pb).astype(o_ref.dtype)


def _merge(att, sg, proj, ga_col, gb_col, w_pa, w_pb, tm, tn):
    m, ka = att.shape
    kb = sg.shape[1]
    n = w_pa.shape[1]
    ga0, gb0 = ga_col // tn, gb_col // tn
    return pl.pallas_call(
        _merge_kernel,
        grid=(m // tm, n // tn),
        in_specs=[pl.BlockSpec((tm, ka), lambda i, j: (i, 0)),
                  pl.BlockSpec((tm, kb), lambda i, j: (i, 0)),
                  pl.BlockSpec((tm, tn), lambda i, j: (i, j + ga0)),
                  pl.BlockSpec((tm, tn), lambda i, j: (i, j + gb0)),
                  pl.BlockSpec((ka, tn), lambda i, j: (0, j)),
                  pl.BlockSpec((kb, tn), lambda i, j: (0, j))],
        out_specs=pl.BlockSpec((tm, tn), lambda i, j: (i, j)),
        out_shape=jax.ShapeDtypeStruct((m, n), BF16),
        compiler_params=_params("parallel", "arbitrary"),
        name="merge",
    )(att, sg, proj, proj, w_pa, w_pb)


def _outproj_kernel(a_ref, w_ref, x_ref, gt_ref, o_ref):
    acc = jnp.dot(a_ref[...], w_ref[...].astype(BF16), preferred_element_type=F32)
    o_ref[...] = x_ref[...] + gt_ref[...] * acc


def _outproj(a, w, x, mod, gt_idx, rows, tm, tn):
    m, k = a.shape
    n = w.shape[1]
    return pl.pallas_call(
        _outproj_kernel,
        grid=(m // tm, n // tn),
        in_specs=[pl.BlockSpec((tm, k), lambda i, j: (i, 0)),
                  pl.BlockSpec((k, tn), lambda i, j: (0, j)),
                  pl.BlockSpec((tm, tn), lambda i, j: (i, j)),
                  rows.spec(tm, tn, gt_idx * (n // tn))],
        out_specs=pl.BlockSpec((tm, tn), lambda i, j: (i, j)),
        out_shape=jax.ShapeDtypeStruct((m, n), F32),
        compiler_params=_params("parallel", "arbitrary"),
        name="outproj",
    )(a, w, x, mod)


def _softmax_parts(s):
    m = jnp.max(s, axis=-1, keepdims=True)
    p = jnp.exp(s - m)
    return m, p, jnp.sum(p, axis=-1, keepdims=True)


def _combine_groups(outs, lses):
    top = functools.reduce(jnp.maximum, lses)
    es = [jnp.exp(l - top) for l in lses]
    tot = functools.reduce(jnp.add, es)
    return functools.reduce(jnp.add, [(e / tot) * o for e, o in zip(es, outs)])


def _attn_prompt_kernel(sl_ref, *refs, seq):
    q_refs, k_refs, v_refs = refs[0:9:3], refs[1:9:3], refs[2:9:3]
    o_ref, og_ref, lg_ref = refs[9], refs[10], refs[11]
    h = pl.program_id(1)
    scale = HEAD_DIM ** -0.5
    span = ATT_SPAN
    back1 = lax.broadcasted_iota(jnp.int32, (span, span), 0) - lax.broadcasted_iota(jnp.int32, (span, span), 1)
    back2 = (lax.broadcasted_iota(jnp.int32, (span, 2 * span), 0) + span
             - lax.broadcasted_iota(jnp.int32, (span, 2 * span), 1))
    for g in range(N_GROUPS):
        dil = DILATIONS[g]
        n_blocks = seq // dil // span
        step_bias = sl_ref[g, h] * float(dil)
        bias1 = jnp.where(back1 >= 0, -(step_bias * back1.astype(F32)), -jnp.inf)
        bias2 = jnp.where((back2 >= 0) & (back2 <= span), -(step_bias * back2.astype(F32)), -jnp.inf)
        blocks = [(res, blk) for blk in range(n_blocks) for res in range(dil)]
        n_first = dil
        rows = [pl.ds(res + dil * span * blk, span) if dil == 1 else pl.ds(res + dil * span * blk, span, stride=dil)
                for res, blk in blocks]
        q = jnp.stack([q_refs[g][r, :] for r in rows]).astype(BF16)
        k_cur = [k_refs[g][r, :].astype(BF16) for r in rows]
        v_cur = [v_refs[g][r, :].astype(BF16) for r in rows]
        if n_blocks == 1:
            k, v = jnp.stack(k_cur), jnp.stack(v_cur)
            bias = bias1[None]
        else:
            prev = [i if blk == 0 else blocks.index((res, blk - 1)) for i, (res, blk) in enumerate(blocks)]
            k = jnp.stack([jnp.concatenate([k_cur[j], k_cur[i]], axis=0) for i, j in enumerate(prev)])
            v = jnp.stack([jnp.concatenate([v_cur[j], v_cur[i]], axis=0) for i, j in enumerate(prev)])
            bias_first = jnp.concatenate([jnp.full((span, span), -jnp.inf, F32), bias1], axis=1)
            bias = jnp.concatenate([jnp.broadcast_to(bias_first[None], (n_first, span, 2 * span)),
                                    jnp.broadcast_to(bias2[None], (len(blocks) - n_first, span, 2 * span))], axis=0)
        s = lax.dot_general(q, k, (((2,), (2,)), ((0,), (0,))), preferred_element_type=F32) * scale + bias
        m, p, den = _softmax_parts(s)
        o = lax.dot_general((p / den).astype(BF16), v, (((2,), (1,)), ((0,), (0,))), preferred_element_type=F32)
        lse = m + jnp.log(den)
        for i, r in enumerate(rows):
            og_ref[g, r, :] = o[i]
            lg_ref[g, r, :] = jnp.broadcast_to(lse[i], (span, HEAD_DIM))

    o_ref[...] = _combine_groups([og_ref[g] for g in range(N_GROUPS)],
                                 [lg_ref[g] for g in range(N_GROUPS)]).astype(o_ref.dtype)


def _attn_prompt(qkv, slopes, batch, seq):
    ncols = qkv.shape[1]
    assert seq % (max(DILATIONS) * ATT_SPAN) == 0, "every residue class needs whole query blocks"
    qkv3 = qkv.reshape(batch, seq, ncols)
    in_specs = [pl.BlockSpec(memory_space=pltpu.SMEM)]
    for g in range(N_GROUPS):
        for part in range(3):
            cb = part * N_HEADS_ATT + g * HEADS_PER_GROUP
            in_specs.append(pl.BlockSpec((None, seq, HEAD_DIM), lambda b, h, cb=cb: (b, 0, cb + h)))
    out = pl.pallas_call(
        functools.partial(_attn_prompt_kernel, seq=seq),
        grid=(batch, HEADS_PER_GROUP),
        in_specs=in_specs,
        out_specs=pl.BlockSpec((None, seq, HEAD_DIM), lambda b, h: (b, 0, h)),
        out_shape=jax.ShapeDtypeStruct((batch, seq, HEADS_PER_GROUP * HEAD_DIM), BF16),
        scratch_shapes=[pltpu.VMEM((N_GROUPS, seq, HEAD_DIM), F32),
                        pltpu.VMEM((N_GROUPS, seq, HEAD_DIM), F32)],
        compiler_params=_params("parallel", "arbitrary"),
        name="attn_prompt",
    )(slopes, *([qkv3] * 9))
    return out.reshape(batch * seq, HEADS_PER_GROUP * HEAD_DIM)


def _attn_sample_kernel(sl_ref, qkv_ref, k0, v0, k1, v1, k2, v2, o_ref, *, dec_seq):
    scale = HEAD_DIM ** -0.5
    att_w = N_HEADS_ATT * HEAD_DIM
    nh = HEADS_PER_GROUP
    caches = ((k0, v0), (k1, v1), (k2, v2))
    batch_nt = (((2,), (2,)), ((0,), (0,)))
    batch_nn = (((2,), (1,)), ((0,), (0,)))
    t_new = lax.broadcasted_iota(jnp.int32, (dec_seq, dec_seq), 0)
    c_new = lax.broadcasted_iota(jnp.int32, (dec_seq, dec_seq), 1)
    outs, lses = [], []
    for g in range(N_GROUPS):
        dil, win = DILATIONS[g], WINDOWS[g]

        def new_rows(part):
            cols = [part * att_w + (g * nh + h) * HEAD_DIM for h in range(nh)]
            return jnp.stack([qkv_ref[:, c:c + HEAD_DIM] for c in cols]).astype(BF16)

        def cached(ref):
            per_head = []
            for h in range(nh):
                head_rows = pl.ds(h, ref.shape[-2] // nh, stride=nh)
                per_head.append(ref[:, head_rows, :].reshape(-1, HEAD_DIM) if g == 2 else ref[head_rows, :])
            return jnp.stack(per_head).astype(BF16)

        q, kn, vn = new_rows(0), new_rows(1), new_rows(2)
        kc, vc = cached(caches[g][0]), cached(caches[g][1])
        rows = kc.shape[1]
        t_c = lax.broadcasted_iota(jnp.int32, (dec_seq, rows), 0)
        c_c = lax.broadcasted_iota(jnp.int32, (dec_seq, rows), 1)
        e_c = ((c_c >> 3) << 4) + (c_c & 7) if g == 2 else c_c
        d_c = win + t_c - e_c
        ok_c = ((d_c & (dil - 1)) == 0) & (d_c <= ATT_SPAN * dil)
        d_n = t_new - c_new
        ok_n = (d_n >= 0) & ((d_n & (dil - 1)) == 0)
        slope = jnp.stack([jnp.full((1, 1), sl_ref[g, h], F32) for h in range(nh)])
        s_c = lax.dot_general(q, kc, batch_nt, preferred_element_type=F32) * scale
        s_n = lax.dot_general(q, kn, batch_nt, preferred_element_type=F32) * scale
        s_c = jnp.where(ok_c[None], s_c - slope * d_c.astype(F32)[None], -jnp.inf)
        s_n = jnp.where(ok_n[None], s_n - slope * d_n.astype(F32)[None], -jnp.inf)
        m = jnp.maximum(jnp.max(s_c, axis=-1, keepdims=True), jnp.max(s_n, axis=-1, keepdims=True))
        p_c = jnp.exp(s_c - m)
        p_n = jnp.exp(s_n - m)
        den = jnp.sum(p_c, axis=-1, keepdims=True) + jnp.sum(p_n, axis=-1, keepdims=True)
        outs.append(lax.dot_general((p_c / den).astype(BF16), vc, batch_nn, preferred_element_type=F32)
                    + lax.dot_general((p_n / den).astype(BF16), vn, batch_nn, preferred_element_type=F32))
        lses.append(m + jnp.log(den))
    combined = _combine_groups(outs, lses)
    for h in range(nh):
        o_ref[:, h * HEAD_DIM:(h + 1) * HEAD_DIM] = combined[h]


def _attn_sample(qkv, slopes, caches, dec_batch, dec_seq):
    att_w = N_HEADS_ATT * HEAD_DIM
    hw = HEADS_PER_GROUP * HEAD_DIM
    ins, specs = [], []
    for g in range(N_GROUPS):
        for c in caches[g]:
            win = c.shape[2]
            if g == 2:
                dil = DILATIONS[g]
                assert dec_seq <= dil and dec_seq % 8 == 0
                ins.append(c.reshape(1, dec_batch, win // dil, dil * HEADS_PER_GROUP, HEAD_DIM))
                specs.append(pl.BlockSpec((None, None, win // dil, dec_seq * HEADS_PER_GROUP, HEAD_DIM),
                                          lambda b: (0, b, 0, 0, 0)))
            else:
                ins.append(c.reshape(1, dec_batch, win * HEADS_PER_GROUP, HEAD_DIM))
                specs.append(pl.BlockSpec((None, None, win * HEADS_PER_GROUP, HEAD_DIM), lambda b: (0, b, 0, 0)))
    return pl.pallas_call(
        functools.partial(_attn_sample_kernel, dec_seq=dec_seq),
        grid=(dec_batch,),
        in_specs=[pl.BlockSpec(memory_space=pltpu.SMEM),
                  pl.BlockSpec((dec_seq, 3 * att_w), lambda b: (b, 0))] + specs,
        out_specs=pl.BlockSpec((dec_seq, hw), lambda b: (b, 0)),
        out_shape=jax.ShapeDtypeStruct((dec_batch * dec_seq, hw), F32),
        compiler_params=_params("parallel"),
        name="attn_sample",
    )(slopes, qkv, *ins)


def _sgu_kernel(*refs, block_rows, parts, want_vn):
    su_refs, sv_refs = refs[:parts], refs[parts:2 * parts]
    lg_ref, lb_ref, w_ref, b_ref, sg_ref = refs[2 * parts:2 * parts + 5]

    def whole(rs):
        return rs[0][...] if parts == 1 else jnp.concatenate([r[...] for r in rs], axis=1)

    u = _gelu(whole(su_refs))
    v = _gelu(whole(sv_refs))
    mu = jnp.mean(v, axis=-1, keepdims=True)
    vc = v - mu
    vn = vc * lax.rsqrt(jnp.mean(vc * vc, axis=-1, keepdims=True) + EPS) * lg_ref[...] + lb_ref[...]
    if want_vn:
        refs[2 * parts + 5][...] = vn
    r = lax.broadcasted_iota(jnp.int32, (CHUNK, CHUNK), 0)
    c = lax.broadcasted_iota(jnp.int32, (CHUNK, CHUNK), 1)
    keep = (c <= r) & ((r // block_rows) == (c // block_rows))
    ch = vn.shape[1] // SG_GROUPS
    for g in range(SG_GROUPS):
        w = jnp.where(keep, w_ref[g], 0.0).astype(BF16)
        mixed = jnp.dot(w, vn[:, g * ch:(g + 1) * ch].astype(BF16), preferred_element_type=F32) + b_ref[:, g:g + 1]
        sg_ref[:, g * ch:(g + 1) * ch] = (u[:, g * ch:(g + 1) * ch] * mixed).astype(sg_ref.dtype)


def _sgu(proj, su_col, sv_col, d_sg, ln_g, ln_b, w, bias_t, block_rows, want_vn):
    m = proj.shape[0]
    out_shape = [jax.ShapeDtypeStruct((m, d_sg), BF16)]
    out_specs = [pl.BlockSpec((CHUNK, d_sg), lambda i: (i, 0))]
    if want_vn:
        out_shape.append(jax.ShapeDtypeStruct((m, d_sg), F32))
        out_specs.append(pl.BlockSpec((CHUNK, d_sg), lambda i: (i, 0)))
    unit = math.gcd(math.gcd(su_col, sv_col), d_sg)
    parts = d_sg // unit
    col_specs = [pl.BlockSpec((CHUNK, unit), lambda i, cb=c0 // unit + k: (i, cb))
                 for c0 in (su_col, sv_col) for k in range(parts)]
    return pl.pallas_call(
        functools.partial(_sgu_kernel, block_rows=block_rows, parts=parts, want_vn=want_vn),
        grid=(m // CHUNK,),
        in_specs=col_specs + [pl.BlockSpec((1, d_sg), lambda i: (0, 0)),
                              pl.BlockSpec((1, d_sg), lambda i: (0, 0)),
                              pl.BlockSpec((SG_GROUPS, CHUNK, CHUNK), lambda i: (0, 0, 0)),
                              pl.BlockSpec((CHUNK, SG_GROUPS), lambda i: (0, 0))],
        out_specs=out_specs,
        out_shape=out_shape,
        compiler_params=_params("parallel"),
        name="sgu",
    )(*([proj] * (2 * parts)), ln_g.reshape(1, d_sg), ln_b.reshape(1, d_sg), w, bias_t)


def _take_top(s, key_id, n_take):
    rows, cols = s.shape
    slot = lax.broadcasted_iota(jnp.int32, (n_take, cols), 0)
    rank = jnp.full((rows, cols), float(n_take), F32)
    top = jnp.zeros((n_take, cols), F32)
    for r in range(n_take):
        m = jnp.max(s, axis=0, keepdims=True)
        first = jnp.min(jnp.where(s == m, key_id, float(rows)), axis=0, keepdims=True)
        sel = key_id == first
        rank = jnp.where(sel, float(r), rank)
        s = jnp.where(sel, -jnp.inf, s)
        top = jnp.where(slot == r, m, top)
    return top, rank


MARK = 2.0 ** 100
MARK_STEP = 2.0 ** 93
MARK_FLOOR = -(2.0 ** 90)


def _take_top_unique(s, n_take):
    cols = s.shape[1]
    slot = lax.broadcasted_iota(jnp.int32, (n_take, cols), 0)
    top = jnp.zeros((n_take, cols), F32)
    for r in range(n_take):
        m = jnp.max(s, axis=0, keepdims=True)
        top = jnp.where(slot == r, m, top)
        s = jnp.where(s == m, -(MARK + r * MARK_STEP), s)
    removed = s <= -MARK
    rank = jnp.where(removed, (-s - MARK) * (1.0 / MARK_STEP), float(n_take))
    return top, rank, jnp.sum(jnp.where(removed, 1.0, 0.0), axis=0, keepdims=True)


def _candidate_pieces(top0, top1):
    tt = top0.shape[1]
    sub = lax.broadcasted_iota(jnp.int32, (8, tt), 0)
    subf = sub.astype(F32)
    pieces, flat = [], []
    for a in range(8):
        n_b = PEER_TOPK // (a + 1)
        for b0 in range(0, n_b, 8):
            c = top0[a:a + 1, :] + top1[b0:b0 + 8, :]
            pieces.append(jnp.where(sub < n_b - b0, c, -jnp.inf))
            flat.append(subf + float(a * PEER_TOPK + b0))
    pieces.append(top0[8:16, :] + top1[0:1, :])
    flat.append((subf + 8.0) * float(PEER_TOPK))
    return pieces, flat


def _write_route(scores, top0, top1, rank0, rank1, pieces, taken, rk_ref, e1_ref, lk_ref, e0_ref):
    nk, tt = rank0.shape
    best = top0[0:1, :] + top1[0:1, :]
    z = functools.reduce(jnp.add, [jnp.sum(jnp.where(t > 0.0, jnp.exp(c - best), 0.0), axis=0, keepdims=True)
                                   for t, c in zip(taken, pieces)])
    counts = [jnp.sum(taken[0] + taken[1], axis=0, keepdims=True)]
    counts += [jnp.sum(taken[a + 1], axis=0, keepdims=True) for a in range(1, 8)]
    counts += [taken[-1][a:a + 1, :] for a in range(8)]
    lk = jnp.zeros((nk, tt), F32)
    for a in range(PEER_TOPK):
        lk = jnp.where(rank0 == float(a), counts[a], lk)
    rk_ref[...] = rank1.astype(rk_ref.dtype)
    e1_ref[...] = jnp.exp(scores[1] - top1[0:1, :]).astype(e1_ref.dtype)
    lk_ref[...] = lk
    e0_ref[...] = jnp.exp(scores[0] - top0[0:1, :]) / z


def _peer_route_kernel(q_ref, sk_ref, *refs):
    tt = q_ref.shape[0]
    half = q_ref.shape[1] // 2
    nk = sk_ref.shape[1]
    if len(refs) > 4:
        t0_ref, t1_ref, c0_ref, c1_ref = refs[0], refs[1], refs[6], refs[7]
        c0_ref[...] = t0_ref[...].astype(BF16)
        c1_ref[...] = t1_ref[...].astype(BF16)
        refs = refs[2:6]
    out_refs = refs
    scores = [lax.dot_general(sk_ref[p].astype(BF16), q_ref[:, p * half:(p + 1) * half].astype(BF16),
                              NT_DIMS, preferred_element_type=F32) for p in range(2)]

    top0, rank0, removed0 = _take_top_unique(scores[0], PEER_TOPK)
    top1, rank1, removed1 = _take_top_unique(scores[1], PEER_TOPK)
    pieces, _ = _candidate_pieces(top0, top1)
    cand = list(pieces)
    for _ in range(PEER_TOPK):
        m = jnp.max(functools.reduce(jnp.maximum, cand), axis=0, keepdims=True)
        cand = [jnp.where(c == m, -MARK, c) for c in cand]
    taken = [jnp.where(c == -MARK, 1.0, 0.0) for c in cand]
    n_taken = functools.reduce(jnp.add, [jnp.sum(t, axis=0, keepdims=True) for t in taken])
    lowest = jnp.minimum(jnp.min(scores[0], axis=0, keepdims=True), jnp.min(scores[1], axis=0, keepdims=True))
    k = float(PEER_TOPK)
    unsure = (removed0 != k) | (removed1 != k) | (n_taken != k) | (lowest < MARK_FLOOR)
    _write_route(scores, top0, top1, rank0, rank1, pieces, taken, *out_refs)

    @pl.when(jnp.sum(jnp.where(unsure, 1.0, 0.0)) > 0.0)
    def _():
        key_id = lax.broadcasted_iota(jnp.int32, (nk, tt), 0).astype(F32)
        xtop0, xrank0 = _take_top(scores[0], key_id, PEER_TOPK)
        xtop1, xrank1 = _take_top(scores[1], key_id, PEER_TOPK)
        xpieces, flat = _candidate_pieces(xtop0, xtop1)
        xcand = list(xpieces)
        xtaken = [jnp.zeros((8, tt), F32) for _ in xpieces]
        big = float(PEER_TOPK * PEER_TOPK)
        for _ in range(PEER_TOPK):
            m = jnp.max(functools.reduce(jnp.maximum, xcand), axis=0, keepdims=True)
            first = jnp.min(functools.reduce(jnp.minimum,
                                             [jnp.where(c == m, f, big) for c, f in zip(xcand, flat)]),
                            axis=0, keepdims=True)
            for i, f in enumerate(flat):
                sel = f == first
                xcand[i] = jnp.where(sel, -jnp.inf, xcand[i])
                xtaken[i] = jnp.where(sel, 1.0, xtaken[i])
        _write_route(scores, xtop0, xtop1, xrank0, xrank1, xpieces, xtaken, *out_refs)


def _peer_route(qp, sub_keys, tt, tables=()):
    m = qp.shape[0]
    nh, _, nk, half = sub_keys.shape
    ospec = pl.BlockSpec((None, nk, tt), lambda i, h: (h, 0, i))
    in_specs = [pl.BlockSpec((tt, 2 * half), lambda i, h: (i, h)),
                pl.BlockSpec((None, 2, nk, half), lambda i, h: (h, 0, 0, 0))]
    out_specs = [ospec] * 4
    out_shape = [jax.ShapeDtypeStruct((nh, nk, m), dt) for dt in (GATE_DTYPE, GATE_DTYPE, F32, F32)]
    if tables:
        rows, width = tables[0].shape
        n_steps = (m // tt) * nh
        rb = rows // n_steps
        assert rows % n_steps == 0 and rb % 16 == 0, (rows, n_steps)
        tspec = pl.BlockSpec((rb, width), lambda i, h: (i * nh + h, 0))
        in_specs += [tspec, tspec]
        out_specs += [tspec, tspec]
        out_shape += [jax.ShapeDtypeStruct((rows, width), BF16)] * 2
    return pl.pallas_call(
        _peer_route_kernel,
        grid=(m // tt, nh),
        in_specs=in_specs,
        out_specs=out_specs,
        out_shape=out_shape,
        compiler_params=_params("arbitrary", "arbitrary"),
        name="peer_route",
    )(qp, sub_keys, *tables)


MXU_COLS = 256


def _peer_dense_kernel(h_ref, u_ref, v_ref, rk_ref, e1_ref, lk_ref, e0_ref, o_ref, act_ref, g_ref, *,
                       n_chunks, n_items):
    s = pl.program_id(0)
    _, tt, ec = act_ref.shape
    nk = rk_ref.shape[1]
    d = v_ref.shape[1]
    first_key = (jnp.clip(s - 1, 0, n_items - 1) % n_chunks) * (ec // nk)
    down_chunk = jnp.clip(s - 2, 0, n_items - 1) % n_chunks

    @pl.when(s == 0)
    def _():
        act_ref[...] = jnp.zeros(act_ref.shape, act_ref.dtype)
        g_ref[...] = jnp.zeros(g_ref.shape, g_ref.dtype)

    @pl.when(down_chunk == 0)
    def _():
        o_ref[...] = jnp.zeros(o_ref.shape, o_ref.dtype)

    n_tiles = d // MXU_COLS
    pieces = [(ii, tb) for ii in range(ec // nk) for tb in range(tt // nk)]
    per_tile = -(-len(pieces) // n_tiles)
    half = ec // 2

    def step(rs, ws):
        key_rows = {}

        def key_row(ref, hd, ii):
            if (id(ref), hd, ii) not in key_rows:
                key_rows[id(ref), hd, ii] = ref[hd, pl.ds(first_key + ii, 1), :].astype(GATE_DTYPE)
            return key_rows[id(ref), hd, ii]

        for n in range(n_tiles):
            if n == 0:
                act_ref[ws, :, :half] = lax.dot_general(h_ref[...], u_ref[:half, :], NT_DIMS,
                                                        preferred_element_type=F32)
            if n == n_tiles // 2:
                act_ref[ws, :, half:] = lax.dot_general(h_ref[...], u_ref[half:, :], NT_DIMS,
                                                        preferred_element_type=F32)
            cols = slice(n * MXU_COLS, (n + 1) * MXU_COLS)
            o_ref[:, cols] += jnp.dot(g_ref[rs], v_ref[:, cols], preferred_element_type=F32)
            for ii, tb in pieces[n * per_tile:(n + 1) * per_tile]:
                toks = slice(tb * nk, (tb + 1) * nk)
                exps = slice(ii * nk, (ii + 1) * nk)
                w = None
                for hd in range(rk_ref.shape[0]):
                    take = rk_ref[hd, :, toks] < key_row(lk_ref, hd, ii)[:, toks]
                    term = (jnp.where(take, e1_ref[hd, :, toks], jnp.zeros((), GATE_DTYPE))
                            * key_row(e0_ref, hd, ii)[:, toks])
                    w = term if w is None else w + term
                g_ref[ws, toks, exps] = (w.astype(F32).T * _gelu(act_ref[rs, toks, exps])).astype(BF16)

    @pl.when(s % 2 == 0)
    def _():
        step(1, 0)

    @pl.when(s % 2 == 1)
    def _():
        step(0, 1)


def _peer_dense(h2, u_bf, v_bf, route, tt, ec):
    m, d = h2.shape
    n_chunks = u_bf.shape[0] // ec
    n_items = (m // tt) * n_chunks
    nh, nk, _ = route[0].shape

    def up_item(s):
        return jnp.minimum(s, n_items - 1)

    def gate_item(s):
        return jnp.clip(s - 1, 0, n_items - 1)

    def down_item(s):
        return jnp.clip(s - 2, 0, n_items - 1)

    rspec = pl.BlockSpec((nh, nk, tt), lambda s: (0, 0, gate_item(s) // n_chunks), pipeline_mode=pl.Buffered(1))
    return pl.pallas_call(
        functools.partial(_peer_dense_kernel, n_chunks=n_chunks, n_items=n_items),
        grid=(n_items + 2,),
        in_specs=[pl.BlockSpec((tt, d), lambda s: (up_item(s) // n_chunks, 0), pipeline_mode=pl.Buffered(1)),
                  pl.BlockSpec((ec, d), lambda s: (up_item(s) % n_chunks, 0)),
                  pl.BlockSpec((ec, d), lambda s: (down_item(s) % n_chunks, 0)),
                  rspec, rspec, rspec, rspec],
        out_specs=pl.BlockSpec((tt, d), lambda s: (down_item(s) // n_chunks, 0)),
        out_shape=jax.ShapeDtypeStruct((m, d), F32),
        scratch_shapes=[pltpu.VMEM((2, tt, ec), F32), pltpu.VMEM((2, tt, ec), BF16)],
        compiler_params=_params("arbitrary"),
        name="peer_dense",
    )(h2, u_bf, v_bf, *route)


def _final_kernel(x_ref, p_ref, gt_ref, g_ref, o_ref):
    x = x_ref[...] + gt_ref[...] * p_ref[...]
    o_ref[...] = x * lax.rsqrt(jnp.mean(x * x, axis=-1, keepdims=True) + EPS) * g_ref[...]


def _final(x, peer, mod, gt_idx, g, rows, tm):
    m, d = x.shape
    return pl.pallas_call(
        _final_kernel,
        grid=(m // tm, 1),
        in_specs=[pl.BlockSpec((tm, d), lambda i, j: (i, 0)),
                  pl.BlockSpec((tm, d), lambda i, j: (i, 0)),
                  rows.spec(tm, d, gt_idx),
                  pl.BlockSpec((1, d), lambda i, j: (0, 0))],
        out_specs=pl.BlockSpec((tm, d), lambda i, j: (i, 0)),
        out_shape=jax.ShapeDtypeStruct((m, d), F32),
        compiler_params=_params("parallel", "arbitrary"),
        name="final",
    )(x, peer, mod, g.reshape(1, d))


def _tile(n, want):
    t = min(n, want)
    assert n % t == 0, (n, want)
    return t


def _layer(x, mod, rows, attend, sgu_block_rows, want_vn, p):
    m, d = x.shape
    d_sg = d // 2
    att_w = N_HEADS_ATT * HEAD_DIM
    tm = _tile(m, 1024)
    tm_row = _tile(m, 256)
    h1 = _normmod(x, p["norm1_g"], mod, 1, 0, rows, tm_row)
    proj = _matmul(h1, p["w_in"], tm, 512, "inproj")
    att = attend(proj)
    sg_out = _sgu(proj, 3 * att_w, 3 * att_w + d_sg, d_sg, p["sg_ln_g"], p["sg_ln_b"],
                  p["sg_w_eff"][sgu_block_rows], p["sg_b_eff"][sgu_block_rows], sgu_block_rows, want_vn)
    sg = sg_out[0]
    merged = _merge(att.astype(BF16), sg, proj, 3 * att_w + 2 * d_sg, 3 * att_w + 2 * d_sg + d,
                    p["w_pa"], p["w_pb"], tm, 512)
    x1 = _outproj(merged, p["w_o"], x, mod, 2, rows, tm, 512)
    h2 = _normmod(x1, p["norm2_g"], mod, 4, 3, rows, tm_row)
    qp = _matmul(h2, p["peer_wq"], tm, 512, "peer_q")
    tt = _tile(m, 512)
    tt_route = _tile(m, 1024)
    if "peer_u_bf" in p:
        route = _peer_route(qp, p["peer_subkeys"], tt_route)
    else:
        *route, p["peer_u_bf"], p["peer_v_bf"] = _peer_route(qp, p["peer_subkeys"], tt_route,
                                                            (p["peer_u"], p["peer_v"]))
    peer = _peer_dense(h2, p["peer_u_bf"], p["peer_v_bf"], route, tt, _tile(p["peer_u_bf"].shape[0], 512))
    y = _final(x1, peer, mod, 5, p["final_g"], rows, tm_row)
    return y, proj, (sg_out[1] if want_vn else None)


def kernel(x_prompt, x_sample, cache_k0, cache_v0, cache_k1, cache_v1, cache_k2, cache_v2, c_prompt, c_sample,
           ada_w, ada_b, norm1_g, norm2_g, w_in, sg_ln_g, sg_ln_b, sg_w, sg_b, w_pa, w_pb, w_o, peer_wq,
           peer_subkeys, peer_u, peer_v, final_g):
    batch, seq, d = x_prompt.shape
    dec_batch, dec_seq, _ = x_sample.shape
    depth = ada_w.shape[0]
    assert depth == 1, "single-layer step"
    att_w = N_HEADS_ATT * HEAD_DIM
    hw = HEADS_PER_GROUP * HEAD_DIM

    slopes = jnp.exp2(-8.0 * jnp.arange(1, N_HEADS_ATT + 1, dtype=F32) / N_HEADS_ATT).reshape(N_GROUPS, HEADS_PER_GROUP)
    sg_wl, sg_bl = sg_w[0], sg_b[0]
    reps = CHUNK // dec_seq
    p = dict(norm1_g=norm1_g[0], norm2_g=norm2_g[0], w_in=w_in[0], sg_ln_g=sg_ln_g[0], sg_ln_b=sg_ln_b[0],
             w_pa=w_pa[0], w_pb=w_pb[0], w_o=w_o[0], peer_wq=peer_wq[0], peer_subkeys=peer_subkeys[0],
             peer_u=peer_u[0], peer_v=peer_v[0], final_g=final_g,
             sg_w_eff={CHUNK: sg_wl, dec_seq: jnp.tile(sg_wl[:, :dec_seq, :dec_seq], (1, reps, reps))},
             sg_b_eff={CHUNK: sg_bl.T, dec_seq: jnp.tile(sg_bl.T[:dec_seq], (reps, 1))})

    c_all = jnp.concatenate([c_prompt, c_sample], axis=0)
    mod = _modulation(c_all, ada_w[0], ada_b[0])
    mod_p = mod[:batch].reshape(batch, 1, N_MOD * d)
    mod_s = jnp.repeat(mod[batch:], dec_seq, axis=0)

    y_p, proj_p, _ = _layer(
        x_prompt.reshape(batch * seq, d), mod_p, _Rows(True, seq),
        lambda proj: _attn_prompt(proj, slopes, batch, seq), CHUNK, False, p)

    caches = [(cache_k0, cache_v0), (cache_k1, cache_v1), (cache_k2, cache_v2)]
    y_s, proj_s, vn_s = _layer(
        x_sample.reshape(dec_batch * dec_seq, d), mod_s, _Rows(False, dec_seq),
        lambda proj: _attn_sample(proj, slopes, caches, dec_batch, dec_seq), dec_seq, True, p)

    outs = [y_p.reshape(batch, seq, d), y_s.reshape(dec_batch, dec_seq, d)]
    kp = proj_p.reshape(batch, seq, -1)
    ks = proj_s.reshape(dec_batch, dec_seq, -1)
    for g in range(N_GROUPS):
        keep = min(WINDOWS[g], seq)
        for src, rows_from in ((kp, seq - keep), (ks, 0)):
            for part in (1, 2):
                c0 = part * att_w + g * hw
                blk = src[:, rows_from:, c0:c0 + hw]
                outs.append(blk.reshape(1, blk.shape[0], blk.shape[1], HEADS_PER_GROUP, HEAD_DIM))
    outs.append(vn_s.reshape(1, dec_batch, dec_seq, d // 2))
    return tuple(outs)
```
